```python
import jax, jax.numpy as jnp
from jax import lax
import numpy as np

D_MODEL = 1024
BATCH = 1
SEQ = 16384
DEPTH = 1

NSA_HEADS = 8
NSA_KV_GROUPS = 2
NSA_HEAD_DIM = 64
NSA_WIDTH = NSA_HEADS * NSA_HEAD_DIM
NSA_KV_DIM = NSA_KV_GROUPS * NSA_HEAD_DIM
CMP_LEN = 32
CMP_STRIDE = 16
CMP_HIDDEN = 64
SEL_BLOCK = 64
SEL_TOPK = 16
WINDOW = 512
Q_BLOCK = 128
FORCE_BONUS = 1.0e4
MLSTM_HEADS = 4
MLSTM_HEAD_DIM = 128
MLSTM_WIDTH = MLSTM_HEADS * MLSTM_HEAD_DIM
MLSTM_CONV = 4
MLSTM_CHUNK = 64
IN_SIZES = (NSA_WIDTH, NSA_KV_DIM, NSA_KV_DIM, NSA_KV_DIM, NSA_KV_DIM, NSA_KV_DIM, NSA_KV_DIM, NSA_HEADS * 3,
            MLSTM_WIDTH, MLSTM_WIDTH, MLSTM_WIDTH, MLSTM_WIDTH, MLSTM_HEADS, MLSTM_HEADS)
D_IN = 3360
MIX_WIDTH = NSA_WIDTH + MLSTM_WIDTH
MEM_LEN = 256
MEM_HEADS = 4
MEM_HEAD_DIM = D_MODEL // MEM_HEADS
D_FF = 2816
FFN_CONV = 3
EPS = 1e-6

kernel_name = 'hymba_nsa_mlstm_convffn_block'


def rmsnorm(x, g):
    xf = x.astype(jnp.float32)
    y = xf * lax.rsqrt(jnp.mean(xf * xf, axis=-1, keepdims=True) + EPS)
    return (y * g.astype(jnp.float32)).astype(x.dtype)


def head_rmsnorm(x, g):
    B, S, NH, dh = x.shape
    return rmsnorm(x, g.reshape(NH, dh)).reshape(B, S, NH * dh)


def causal_dwconv(x, w):
    K, C = w.shape
    return lax.conv_general_dilated(x, w[:, None, :].astype(x.dtype), window_strides=(1,), padding=[(K - 1, 0)],
                                    dimension_numbers=('NWC', 'WIO', 'NWC'), feature_group_count=C)


def split_cols(a, sizes):
    return jnp.split(a, np.cumsum(np.array(sizes))[:-1].tolist(), axis=-1)


def masked_softmax(s, mask):
    s = jnp.where(mask, s.astype(jnp.float32), -1e30)
    return jax.nn.softmax(s, axis=-1) * mask


def compress_blocks(kv, pos, w1, w2):
    S, G, dh = kv.shape
    n_cmp = (S - CMP_LEN) // CMP_STRIDE + 1
    idx = jnp.arange(n_cmp)[:, None] * CMP_STRIDE + jnp.arange(CMP_LEN)[None, :]
    blocks = kv[idx] + pos[None, :, None, :]
    flat = blocks.transpose(0, 2, 1, 3).reshape(n_cmp, G, CMP_LEN * dh)
    return jax.nn.gelu(flat @ w1, approximate=False) @ w2


def nsa_mixer(q, k_cmp, v_cmp, k_sel, v_sel, k_win, v_win, gates, pos_k, w1_k, w2_k, pos_v, w1_v, w2_v):
    S = q.shape[0]
    G, HPG, dh = NSA_KV_GROUPS, NSA_HEADS // NSA_KV_GROUPS, NSA_HEAD_DIM
    scale = dh ** -0.5
    kc = compress_blocks(k_cmp, pos_k, w1_k, w2_k)
    vc = compress_blocks(v_cmp, pos_v, w1_v, w2_v)
    n_cmp = kc.shape[0]
    cmp_start = jnp.arange(n_cmp) * CMP_STRIDE
    cmp_end = cmp_start + (CMP_LEN - 1)
    n_sel = S // SEL_BLOCK
    top_k = min(SEL_TOPK, n_sel)
    sel_start = jnp.arange(n_sel) * SEL_BLOCK
    overlap = ((cmp_start[:, None] < sel_start[None, :] + SEL_BLOCK) &
               (cmp_start[:, None] + CMP_LEN > sel_start[None, :])).astype(jnp.float32)
    ks_blocks = k_sel.reshape(n_sel, SEL_BLOCK, G, dh).transpose(2, 0, 1, 3)
    vs_blocks = v_sel.reshape(n_sel, SEL_BLOCK, G, dh).transpose(2, 0, 1, 3)
    kw_pad = jnp.pad(k_win, ((WINDOW, 0), (0, 0), (0, 0)))
    vw_pad = jnp.pad(v_win, ((WINDOW, 0), (0, 0), (0, 0)))
    g_idx = jnp.arange(G)[None, :, None]
    blk_ids = jnp.arange(n_sel)[None, None, :]
    tok_off = jnp.arange(SEL_BLOCK)
    win_off = jnp.arange(Q_BLOCK + WINDOW) - WINDOW

    def block(qb):
        t0 = qb * Q_BLOCK
        tpos = t0 + jnp.arange(Q_BLOCK)
        qg = lax.dynamic_slice_in_dim(q, t0, Q_BLOCK, 0).reshape(Q_BLOCK, G, HPG, dh) * scale
        gb = lax.dynamic_slice_in_dim(gates, t0, Q_BLOCK, 0).reshape(Q_BLOCK, G, HPG, 3)
        mask_c = (cmp_end[None, :] <= tpos[:, None])[:, None, None, :]
        p_c = masked_softmax(jnp.einsum('tghd,cgd->tghc', qg, kc), mask_c)
        o_c = jnp.einsum('tghc,cgd->tghd', p_c.astype(vc.dtype), vc)
        imp = jnp.einsum('tghc,cb->tgb', p_c, overlap)
        cur = (tpos // SEL_BLOCK)[:, None, None]
        forced = (blk_ids == 0) | (blk_ids == cur) | (blk_ids == cur - 1)
        score = jnp.where(blk_ids <= cur, imp + jnp.where(forced, FORCE_BONUS, 0.0), -FORCE_BONUS)
        _, sel = lax.top_k(score, top_k)
        kg = ks_blocks[g_idx, sel]
        vg = vs_blocks[g_idx, sel]
        tok = sel[..., None] * SEL_BLOCK + tok_off
        mask_s = (tok <= tpos[:, None, None, None]).reshape(Q_BLOCK, G, 1, top_k * SEL_BLOCK)
        s_s = jnp.einsum('tghd,tgkpd->tghkp', qg, kg).reshape(Q_BLOCK, G, HPG, top_k * SEL_BLOCK)
        p_s = masked_softmax(s_s, mask_s).reshape(Q_BLOCK, G, HPG, top_k, SEL_BLOCK)
        o_s = jnp.einsum('tghkp,tgkpd->tghd', p_s.astype(vg.dtype), vg)
        kw = lax.dynamic_slice_in_dim(kw_pad, t0, Q_BLOCK + WINDOW, 0)
        vw = lax.dynamic_slice_in_dim(vw_pad, t0, Q_BLOCK + WINDOW, 0)
        kpos = t0 + win_off
        mask_w = ((kpos[None, :] <= tpos[:, None]) & (kpos[None, :] > tpos[:, None] - WINDOW) &
                  (kpos[None, :] >= 0))[:, None, None, :]
        p_w = masked_softmax(jnp.einsum('tghd,wgd->tghw', qg, kw), mask_w)
        o_w = jnp.einsum('tghw,wgd->tghd', p_w.astype(vw.dtype), vw)
        o = gb[..., 0:1] * o_c + gb[..., 1:2] * o_s + gb[..., 2:3] * o_w
        return o.reshape(Q_BLOCK, NSA_HEADS, dh)

    out = lax.map(block, jnp.arange(S // Q_BLOCK))
    return out.reshape(S, NSA_HEADS, dh)


def mlstm_chunkwise(q, k, v, i_pre, f_pre):
    B, S, NH, dh = q.shape
    L = MLSTM_CHUNK
    nc = S // L
    f32 = jnp.float32
    to_c = lambda a: a.astype(f32).reshape(B, nc, L, NH, -1).transpose(1, 0, 3, 2, 4)
    to_cg = lambda a: a.astype(f32).reshape(B, nc, L, NH).transpose(1, 0, 3, 2)
    qc, kc, vc = to_c(q * dh ** -0.5), to_c(k), to_c(v)
    ic, lfc = to_cg(i_pre), to_cg(jax.nn.log_sigmoid(f_pre.astype(f32)))
    causal = jnp.tril(jnp.ones((L, L), dtype=bool))

    def step(carry, inp):
        C, n, m = carry
        qq, kk, vv, ii, lf = inp
        b = jnp.cumsum(lf, axis=-1)
        D = jnp.where(causal, b[..., :, None] - b[..., None, :] + ii[..., None, :], -jnp.inf)
        m_inter = b + m[..., None]
        m_t = jnp.maximum(jnp.max(D, axis=-1), m_inter)
        a_inter = jnp.exp(m_inter - m_t)
        sc = jnp.einsum('bhtd,bhsd->bhts', qq, kk) * jnp.exp(D - m_t[..., None])
        num = a_inter[..., None] * jnp.einsum('bhtd,bhde->bhte', qq, C) + jnp.einsum('bhts,bhse->bhte', sc, vv)
        den = a_inter * jnp.einsum('bhtd,bhd->bht', qq, n) + jnp.sum(sc, axis=-1)
        h = num / jnp.maximum(jnp.abs(den), jnp.exp(-m_t))[..., None]
        bL = b[..., -1]
        g_s = bL[..., None] - b + ii
        m_new = jnp.maximum(bL + m, jnp.max(g_s, axis=-1))
        w_s = jnp.exp(g_s - m_new[..., None])
        decay = jnp.exp(bL + m - m_new)
        C_new = decay[..., None, None] * C + jnp.einsum('bhs,bhsd,bhse->bhde', w_s, kk, vv)
        n_new = decay[..., None] * n + jnp.einsum('bhs,bhsd->bhd', w_s, kk)
        return (C_new, n_new, m_new), h

    init = (jnp.zeros((B, NH, dh, dh), f32), jnp.zeros((B, NH, dh), f32), jnp.zeros((B, NH), f32))
    _, h = lax.scan(step, init, (qc, kc, vc, ic, lfc))
    return h.transpose(1, 0, 3, 2, 4).reshape(B, S, NH, dh).astype(q.dtype)


def setup_inputs(seed: int = 0) -> dict:
    key = jax.random.key(seed)
    ks = jax.random.split(key, 32)
    f32 = jnp.float32
    nrm = lambda k, shape, s: jax.random.normal(k, shape, f32) * s
    gain = lambda k, shape: 1.0 + 0.02 * jax.random.normal(k, shape, f32)
    L = DEPTH
    return {
        'x': nrm(ks[0], (BATCH, SEQ, D_MODEL), 1.0),
        'mem': nrm(ks[1], (BATCH, MEM_LEN, D_MODEL), 1.0),
        'g_mix': gain(ks[2], (L, D_MODEL)),
        'w_in': nrm(ks[3], (L, D_MODEL, D_IN), D_MODEL ** -0.5),
        'cmp_pos_k': nrm(ks[4], (L, CMP_LEN, NSA_HEAD_DIM), 0.02),
        'cmp_w1_k': nrm(ks[5], (L, CMP_LEN * NSA_HEAD_DIM, CMP_HIDDEN), (CMP_LEN * NSA_HEAD_DIM) ** -0.5),
        'cmp_w2_k': nrm(ks[6], (L, CMP_HIDDEN, NSA_HEAD_DIM), CMP_HIDDEN ** -0.5),
        'cmp_pos_v': nrm(ks[7], (L, CMP_LEN, NSA_HEAD_DIM), 0.02),
        'cmp_w1_v': nrm(ks[8], (L, CMP_LEN * NSA_HEAD_DIM, CMP_HIDDEN), (CMP_LEN * NSA_HEAD_DIM) ** -0.5),
        'cmp_w2_v': nrm(ks[9], (L, CMP_HIDDEN, NSA_HEAD_DIM), CMP_HIDDEN ** -0.5),
        'mlstm_conv_w': nrm(ks[10], (L, MLSTM_CONV, 2 * MLSTM_WIDTH), MLSTM_CONV ** -0.5),
        'mlstm_b_i': nrm(ks[11], (L, MLSTM_HEADS), 0.1),
        'mlstm_b_f': jnp.linspace(3.0, 6.0, MLSTM_HEADS, dtype=f32)[None, :] + nrm(ks[12], (L, MLSTM_HEADS), 0.1),
        'g_head_nsa': gain(ks[13], (L, NSA_WIDTH)),
        'g_head_mlstm': gain(ks[14], (L, MLSTM_WIDTH)),
        'w_out': nrm(ks[15], (L, MIX_WIDTH, D_MODEL), MIX_WIDTH ** -0.5),
        'g_mem_q': gain(ks[16], (L, D_MODEL)),
        'g_mem_kv': gain(ks[17], (L, D_MODEL)),
        'w_mem_q': nrm(ks[18], (L, D_MODEL, D_MODEL), D_MODEL ** -0.5),
        'w_mem_k': nrm(ks[19], (L, D_MODEL, D_MODEL), D_MODEL ** -0.5),
        'w_mem_v': nrm(ks[20], (L, D_MODEL, D_MODEL), D_MODEL ** -0.5),
        'w_mem_o': nrm(ks[21], (L, D_MODEL, D_MODEL), D_MODEL ** -0.5),
        'g_ffn': gain(ks[22], (L, D_MODEL)),
        'w_up': nrm(ks[23], (L, D_MODEL, 2 * D_FF), D_MODEL ** -0.5),
        'ffn_conv_w': nrm(ks[24], (L, FFN_CONV, 2 * D_FF), FFN_CONV ** -0.5),
        'w_down': nrm(ks[25], (L, D_FF, D_MODEL), D_FF ** -0.5),
        'g_final': gain(ks[26], (D_MODEL,)),
    }


def reference(x, mem, g_mix, w_in, cmp_pos_k, cmp_w1_k, cmp_w2_k, cmp_pos_v, cmp_w1_v, cmp_w2_v,
              mlstm_conv_w, mlstm_b_i, mlstm_b_f, g_head_nsa, g_head_mlstm, w_out,
              g_mem_q, g_mem_kv, w_mem_q, w_mem_k, w_mem_v, w_mem_o,
              g_ffn, w_up, ffn_conv_w, w_down, g_final):
    B, S, _ = x.shape
    M = mem.shape[1]
    nsa_batched = jax.vmap(nsa_mixer, in_axes=(0,) * 8 + (None,) * 6)
    for l in range(DEPTH):
        h = rmsnorm(x, g_mix[l])
        (nq, nkc, nvc, nks, nvs, nkw, nvw, ngt, mq, mk, mv, mo, mi, mf) = split_cols(h @ w_in[l], IN_SIZES)
        kv4 = lambda a: a.reshape(B, S, NSA_KV_GROUPS, NSA_HEAD_DIM)
        o_nsa = nsa_batched(nq.reshape(B, S, NSA_HEADS, NSA_HEAD_DIM), kv4(nkc), kv4(nvc), kv4(nks), kv4(nvs),
                            kv4(nkw), kv4(nvw), jax.nn.sigmoid(ngt).reshape(B, S, NSA_HEADS, 3),
                            cmp_pos_k[l], cmp_w1_k[l], cmp_w2_k[l], cmp_pos_v[l], cmp_w1_v[l], cmp_w2_v[l])
        qk = jax.nn.silu(causal_dwconv(jnp.concatenate([mq, mk], axis=-1), mlstm_conv_w[l]))
        mq, mk = jnp.split(qk, 2, axis=-1)
        hd4 = lambda a: a.reshape(B, S, MLSTM_HEADS, MLSTM_HEAD_DIM)
        h_m = mlstm_chunkwise(hd4(mq), hd4(mk), hd4(mv), mi + mlstm_b_i[l], mf + mlstm_b_f[l])
        mixed = jnp.concatenate([head_rmsnorm(o_nsa, g_head_nsa[l]),
                                 jax.nn.sigmoid(mo) * head_rmsnorm(h_m, g_head_mlstm[l])], axis=-1)
        x = x + mixed @ w_out[l]
        hq = rmsnorm(x, g_mem_q[l])
        hm = rmsnorm(mem, g_mem_kv[l])
        q = (hq @ w_mem_q[l]).reshape(B, S, MEM_HEADS, MEM_HEAD_DIM)
        k = (hm @ w_mem_k[l]).reshape(B, M, MEM_HEADS, MEM_HEAD_DIM)
        v = (hm @ w_mem_v[l]).reshape(B, M, MEM_HEADS, MEM_HEAD_DIM)
        s = jnp.einsum('bshd,bmhd->bhsm', q, k).astype(jnp.float32) * (MEM_HEAD_DIM ** -0.5)
        p = jax.nn.softmax(s, axis=-1).astype(v.dtype)
        x = x + jnp.einsum('bhsm,bmhd->bshd', p, v).reshape(B, S, D_MODEL) @ w_mem_o[l]
        hf = rmsnorm(x, g_ffn[l])
        u = causal_dwconv(hf @ w_up[l], ffn_conv_w[l])
        a, b = jnp.split(u, 2, axis=-1)
        x = x + (jax.nn.gelu(a, approximate=False) * b) @ w_down[l]
    return rmsnorm(x, g_final)
```

```python
import functools

import jax
import jax.numpy as jnp
from jax import lax
from jax.experimental import pallas as pl
from jax.experimental.pallas import tpu as pltpu

f32 = jnp.float32
bf16 = jnp.bfloat16

D_MODEL = 1024
NSA_HEADS = 8
NSA_GROUPS = 2
NSA_HPG = NSA_HEADS // NSA_GROUPS
NSA_DH = 64
CMP_LEN = 32
CMP_STRIDE = 16
SEL_BLOCK = 64
SEL_TOPK = 16
WINDOW = 512
FORCE_BONUS = 1.0e4
M_HEADS = 4
M_DH = 128
M_WIDTH = M_HEADS * M_DH
M_CONV = 4
M_CHUNK = 64
MEM_HEADS = 4
MEM_DH = D_MODEL // MEM_HEADS
D_FF = 2816
F_CONV = 3
EPS = 1e-6

LANE = 128
SUBLANE = 8
NEG = -1e30
VMEM_LIMIT = 56 * 1024 * 1024

A_Q, A_KS, A_VS, A_KW, A_VW, NA = 0, 1024, 1152, 1280, 1408, 1536
B_MQ, B_MV, B_MO, B_KC, B_VC, B_G0, B_IF, NB = 0, 1024, 1536, 2048, 2176, 2304, 2560, 2688


def _dot(a, b):
    return jnp.dot(a, b, preferred_element_type=f32)


def _rms(x, g):
    return x * lax.rsqrt(jnp.mean(x * x, axis=-1, keepdims=True) + EPS) * g


def _gelu(x):
    return 0.5 * x * (1.0 + lax.erf(x * 0.7071067811865476))


def _params(*sem):
    return pltpu.CompilerParams(dimension_semantics=sem, vmem_limit_bytes=VMEM_LIMIT)


def _const_spec(shape):
    nd = len(shape)
    return pl.BlockSpec(shape, lambda *_: (0,) * nd, pipeline_mode=pl.Buffered(1))


def _inproj_kernel(x_ref, g_ref, w_ref, pa_ref, pb_ref):
    h = _rms(x_ref[...], g_ref[...]).astype(bf16)
    pa_ref[:, A_Q:A_KS] = (_dot(h, w_ref[:, A_Q:A_KS]) * (NSA_DH ** -0.5)).astype(bf16)
    pa_ref[:, A_KS:NA] = _dot(h, w_ref[:, A_KS:NA]).astype(bf16)
    for c0 in range(0, NB, 512):
        c1 = min(c0 + 512, NB)
        pb_ref[:, c0:c1] = _dot(h, w_ref[:, NA + c0:NA + c1])


def _inproj(x, g, w, tm=512):
    s = x.shape[0]
    return pl.pallas_call(
        _inproj_kernel,
        grid=(s // tm,),
        in_specs=[pl.BlockSpec((tm, D_MODEL), lambda i: (i, 0)),
                  _const_spec((1, D_MODEL)),
                  _const_spec((D_MODEL, NA + NB))],
        out_specs=[pl.BlockSpec((tm, NA), lambda i: (i, 0)),
                   pl.BlockSpec((tm, NB), lambda i: (i, 0))],
        out_shape=[jax.ShapeDtypeStruct((s, NA), bf16), jax.ShapeDtypeStruct((s, NB), f32)],
        compiler_params=_params("parallel"),
        name="inproj",
    )(x, g, w)


def _compress_kernel(h_ref, pos_ref, w1_ref, w2_ref, o_ref):
    hh = h_ref[0, 0]
    pos = pos_ref[0]
    half = hh.shape[1]
    n = hh.shape[0]
    top = (hh + pos[:, :half]).astype(bf16)
    bot = (hh + pos[:, half:]).astype(bf16)
    a = _dot(top, w1_ref[0, :half, :])
    b = _dot(bot, w1_ref[0, half:, :])
    pre = a + pltpu.roll(b, n - 1, 0)
    o_ref[0, 0] = _dot(_gelu(pre).astype(bf16), w2_ref[0])


def _compress(hflat, pos, w1, w2):
    _, _, n, wd = hflat.shape
    return pl.pallas_call(
        _compress_kernel,
        grid=(2, NSA_GROUPS),
        in_specs=[pl.BlockSpec((1, 1, n, wd), lambda a, g: (a, g, 0, 0)),
                  pl.BlockSpec((1, 1, 2 * wd), lambda a, g: (a, 0, 0)),
                  pl.BlockSpec((1, 2 * wd, NSA_DH), lambda a, g: (a, 0, 0)),
                  pl.BlockSpec((1, NSA_DH, NSA_DH), lambda a, g: (a, 0, 0))],
        out_specs=pl.BlockSpec((1, 1, n, NSA_DH), lambda a, g: (a, g, 0, 0)),
        out_shape=jax.ShapeDtypeStruct((2, NSA_GROUPS, n, NSA_DH), f32),
        compiler_params=_params("parallel", "parallel"),
        name="compress",
    )(hflat, pos, w1, w2)


def _stack_heads(q_ref, g):
    return jnp.concatenate(
        [q_ref[:, (g * NSA_HPG + hh) * LANE:(g * NSA_HPG + hh + 1) * LANE] for hh in range(NSA_HPG)], axis=0)


def _cmp_kernel(q_ref, kct_ref, vc_ref, ov_ref, oc_ref, sel_ref, *, tq):
    ncmp = kct_ref.shape[2]
    nsel = ov_ref.shape[1]
    t0 = pl.program_id(0) * tq
    rows = NSA_HPG * tq
    tpos_r = t0 + lax.rem(lax.broadcasted_iota(jnp.int32, (rows, 1), 0), tq)
    cend = lax.broadcasted_iota(jnp.int32, (1, ncmp), 1) * CMP_STRIDE + (CMP_LEN - 1)
    vis = cend <= tpos_r
    tpos = t0 + lax.broadcasted_iota(jnp.int32, (tq, 1), 0)
    cur = lax.shift_right_logical(tpos, 6)
    blk = lax.broadcasted_iota(jnp.int32, (1, nsel), 1)
    blkf = lax.broadcasted_iota(jnp.int32, (tq, nsel), 1).astype(f32)
    forced = (blk == 0) | (blk == cur) | (blk == cur - 1)
    for g in range(NSA_GROUPS):
        q = _stack_heads(q_ref, g)
        s = jnp.where(vis, _dot(q, kct_ref[g]), NEG)
        m = jnp.max(s, axis=-1, keepdims=True)
        e = jnp.where(vis, jnp.exp(s - m), 0.0)
        l = jnp.sum(e, axis=-1, keepdims=True)
        p = e * (1.0 / jnp.where(l > 0.0, l, 1.0))
        o = _dot(p.astype(bf16), vc_ref[g])
        for hh in range(NSA_HPG):
            c0 = (g * NSA_HPG + hh) * LANE
            oc_ref[:, c0:c0 + LANE] = o[hh * tq:(hh + 1) * tq]
        psum = p[0:tq] + p[tq:2 * tq] + p[2 * tq:3 * tq] + p[3 * tq:4 * tq]
        imp = _dot(psum.astype(bf16), ov_ref[...])
        score = jnp.where(blk <= cur, imp + jnp.where(forced, FORCE_BONUS, 0.0), -FORCE_BONUS)
        selm = jnp.zeros((tq, nsel), f32)
        for _ in range(SEL_TOPK):
            mx = jnp.max(score, axis=-1, keepdims=True)
            j = jnp.min(jnp.where(score == mx, blkf, float(nsel)), axis=-1, keepdims=True)
            hit = blkf == j
            selm = jnp.where(hit, 1.0, selm)
            score = jnp.where(hit, -jnp.inf, score)
        sel_ref[g] = selm


def _cmp_attention(pa, kct, vc, ov, tq=128):
    s = pa.shape[0]
    ncmp, nsel = ov.shape
    return pl.pallas_call(
        functools.partial(_cmp_kernel, tq=tq),
        grid=(s // tq,),
        in_specs=[pl.BlockSpec((tq, NSA_HEADS * LANE), lambda i: (i, 0)),
                  _const_spec((NSA_GROUPS, LANE, ncmp)),
                  _const_spec((NSA_GROUPS, ncmp, LANE)),
                  _const_spec((ncmp, nsel))],
        out_specs=[pl.BlockSpec((tq, NSA_HEADS * LANE), lambda i: (i, 0)),
                   pl.BlockSpec((NSA_GROUPS, tq, nsel), lambda i: (0, i, 0))],
        out_shape=[jax.ShapeDtypeStruct((s, NSA_HEADS * LANE), f32),
                   jax.ShapeDtypeStruct((NSA_GROUPS, s, nsel), f32)],
        compiler_params=_params("parallel"),
        name="cmp_attn",
    )(pa, kct, vc, ov)


def _selwin_kernel(q_ref, oc_ref, sel_ref, gate_ref, gh_ref, kst_ref, vs_ref, kwt_ref, vw_ref, out_ref, *, tq, tk):
    nsel = sel_ref.shape[2]
    nblk = tk // SEL_BLOCK
    t0 = pl.program_id(1) * tq
    rows = NSA_HPG * tq
    tpos_r = t0 + lax.rem(lax.broadcasted_iota(jnp.int32, (rows, 1), 0), tq)
    q4 = [q_ref[:, hh * LANE:(hh + 1) * LANE] for hh in range(NSA_HPG)]
    sel = sel_ref[0]
    lane = lax.broadcasted_iota(jnp.int32, (tq, LANE), 1)
    bias_lanes = (lane >= NSA_DH) & (lane < NSA_DH + nblk)

    def q_with_bias(kt):
        sh = lax.rem(NSA_DH + nsel - kt * nblk, nsel)
        r = pltpu.roll(sel, sh, 1)[:, :LANE]
        bias = ((r - 1.0) * -NEG).astype(bf16)
        return jnp.concatenate([jnp.where(bias_lanes, bias, qh) for qh in q4], axis=0)

    def tile(kt, m, acc, causal):
        st = pl.multiple_of(kt * tk, tk)
        s = _dot(q_with_bias(kt), kst_ref[0, :, pl.ds(st, tk)])
        if causal:
            kpos = st + lax.broadcasted_iota(jnp.int32, (1, tk), 1)
            s = jnp.where(kpos <= tpos_r, s, NEG)
        mn = jnp.maximum(m, jnp.max(s, axis=-1, keepdims=True))
        p = jnp.exp(s - mn)
        acc = jnp.exp(m - mn) * acc + _dot(p.astype(bf16), vs_ref[0, pl.ds(st, tk), :])
        return mn, acc

    last = (t0 + tq - 1) // tk
    m0 = jnp.full((rows, 1), NEG, f32)
    a0 = jnp.zeros((rows, LANE), f32)
    m1, a1 = lax.fori_loop(0, last, lambda kt, c: tile(kt, c[0], c[1], False), (m0, a0))
    _, acc_s = tile(last, m1, a1, True)

    wlen = tq + WINDOW
    ws = pl.multiple_of(jnp.maximum(t0 - WINDOW, 0), LANE)
    qs = jnp.concatenate(q4, axis=0)
    sw = _dot(qs, kwt_ref[0, :, pl.ds(ws, wlen)])
    kpos = ws + lax.broadcasted_iota(jnp.int32, (1, wlen), 1)
    wmask = (kpos <= tpos_r) & (kpos > tpos_r - WINDOW)
    sw = jnp.where(wmask, sw, NEG)
    mw = jnp.max(sw, axis=-1, keepdims=True)
    pw = jnp.where(wmask, jnp.exp(sw - mw), 0.0)
    acc_w = _dot(pw.astype(bf16), vw_ref[0, pl.ds(ws, wlen), :])

    gates = jax.nn.sigmoid(gate_ref[...])
    valid = lane < NSA_DH
    for hh in range(NSA_HPG):
        r0, r1 = hh * tq, (hh + 1) * tq
        o_s = acc_s[r0:r1] / acc_s[r0:r1, NSA_DH:NSA_DH + 1]
        o_w = acc_w[r0:r1] / acc_w[r0:r1, NSA_DH:NSA_DH + 1]
        o_c = oc_ref[:, hh * LANE:(hh + 1) * LANE]
        o = (gates[:, 3 * hh:3 * hh + 1] * o_c + gates[:, 3 * hh + 1:3 * hh + 2] * o_s
             + gates[:, 3 * hh + 2:3 * hh + 3] * o_w)
        o = jnp.where(valid, o, 0.0)
        ms = jnp.sum(o * o, axis=-1, keepdims=True) * (1.0 / NSA_DH)
        y = o * lax.rsqrt(ms + EPS) * gh_ref[:, hh * LANE:(hh + 1) * LANE]
        out_ref[:, hh * LANE:(hh + 1) * LANE] = y.astype(bf16)


def _selwin_attention(pa, pb, oc, sel, gh_pad, kst, vs, kwt, vw, tq=128, tk=1024):
    s = pa.shape[0]
    nsel = sel.shape[2]
    gw = NSA_HPG * LANE
    kspec = pl.BlockSpec((1, LANE, s), lambda g, i: (g, 0, 0), pipeline_mode=pl.Buffered(1))
    vspec = pl.BlockSpec((1, s, LANE), lambda g, i: (g, 0, 0), pipeline_mode=pl.Buffered(1))
    return pl.pallas_call(
        functools.partial(_selwin_kernel, tq=tq, tk=tk),
        grid=(NSA_GROUPS, s // tq),
        in_specs=[pl.BlockSpec((tq, gw), lambda g, i: (i, g)),
                  pl.BlockSpec((tq, gw), lambda g, i: (i, g)),
                  pl.BlockSpec((1, tq, nsel), lambda g, i: (g, i, 0)),
                  pl.BlockSpec((tq, LANE), lambda g, i: (i, B_G0 // LANE + g)),
                  pl.BlockSpec((1, gw), lambda g, i: (0, g)),
                  kspec, vspec, kspec, vspec],
        out_specs=pl.BlockSpec((tq, gw), lambda g, i: (i, g)),
        out_shape=jax.ShapeDtypeStruct((s, NSA_HEADS * LANE), bf16),
        compiler_params=_params("parallel", "parallel"),
        name="selwin_attn",
    )(pa, oc, sel, pb, gh_pad, kst, vs, kwt, vw)


def _mlstm_kernel(qk_ref, v_ref, og_ref, if_ref, cw_ref, bias_ref, gh_ref, out_ref, buf, cst, mst, *, tm):
    @pl.when(pl.program_id(0) == 0)
    def _():
        buf[0:SUBLANE, :] = jnp.zeros((SUBLANE, 2 * M_WIDTH), f32)
        cst[...] = jnp.zeros(cst.shape, f32)
        mst[...] = jnp.zeros(mst.shape, f32)

    qk = qk_ref[...]
    buf[SUBLANE:SUBLANE + tm, :] = qk
    conv = cw_ref[M_CONV - 1:M_CONV, :] * qk
    for k in range(M_CONV - 1):
        off = SUBLANE - (M_CONV - 1) + k
        conv = conv + cw_ref[k:k + 1, :] * buf[off:off + tm, :]
    buf[0:SUBLANE, :] = qk[tm - SUBLANE:tm]
    qkc = conv * jax.nn.sigmoid(conv)
    qs = (qkc[:, :M_WIDTH] * (M_DH ** -0.5)).astype(bf16)
    ks = qkc[:, M_WIDTH:]
    vb = v_ref[...].astype(bf16)
    ogate = jax.nn.sigmoid(og_ref[...])

    pre = if_ref[...] + bias_ref[...]
    lane = lax.broadcasted_iota(jnp.int32, (M_CHUNK, LANE), 1)
    rowi = lax.broadcasted_iota(jnp.int32, (M_CHUNK, LANE), 0)
    logf = jnp.minimum(pre, 0.0) - jnp.log1p(jnp.exp(-jnp.abs(pre)))
    tri = (lax.broadcasted_iota(jnp.int32, (M_CHUNK, M_CHUNK), 0)
           >= lax.broadcasted_iota(jnp.int32, (M_CHUNK, M_CHUNK), 1))
    ones_col = (lax.broadcasted_iota(jnp.int32, (M_CHUNK, LANE), 1) == 0).astype(bf16)

    for c in range(tm // M_CHUNK):
        r0, r1 = c * M_CHUNK, (c + 1) * M_CHUNK
        cum = logf[r0:r1]
        sft = 1
        while sft < M_CHUNK:
            cum = cum + jnp.where(rowi >= sft, pltpu.roll(cum, sft, 0), 0.0)
            sft *= 2
        comb = jnp.where(lane < M_HEADS, pre[r0:r1], cum)
        comb_t = comb.T
        for h in range(M_HEADS):
            hs = slice(h * M_DH, (h + 1) * M_DH)
            bcol = comb[:, M_HEADS + h:M_HEADS + h + 1]
            icol = comb[:, h:h + 1]
            brow = comb_t[M_HEADS + h:M_HEADS + h + 1, :]
            irow = comb_t[h:h + 1, :]
            mprev = mst[0:1, h:h + 1]
            dmat = jnp.where(tri, bcol - brow + irow, -jnp.inf)
            m_inter = bcol + mprev
            m_t = jnp.maximum(jnp.max(dmat, axis=-1, keepdims=True), m_inter)
            a_inter = jnp.exp(m_inter - m_t)
            qh = qs[r0:r1, hs]
            kh = ks[r0:r1, hs]
            v_aug = jnp.concatenate([vb[r0:r1, hs], ones_col], axis=1)
            qk_s = lax.dot_general(qh, kh.astype(bf16), (((1,), (1,)), ((), ())), preferred_element_type=f32)
            sc = qk_s * jnp.exp(dmat - m_t)
            tot = a_inter * _dot(qh, cst[h].astype(bf16)) + _dot(sc.astype(bf16), v_aug)
            den = tot[:, M_DH:M_DH + 1]
            hout = tot[:, :M_DH] / jnp.maximum(jnp.abs(den), jnp.exp(-m_t))
            b_last = bcol[M_CHUNK - 1:M_CHUNK, :]
            g_s = b_last - bcol + icol
            m_new = jnp.maximum(b_last + mprev, jnp.max(g_s, axis=0, keepdims=True))
            w_s = jnp.exp(g_s - m_new)
            decay = jnp.exp(b_last + mprev - m_new)
            upd = lax.dot_general((kh * w_s).astype(bf16), v_aug, (((0,), (0,)), ((), ())),
                                  preferred_element_type=f32)
            cst[h] = decay * cst[h] + upd
            mst[0:1, h:h + 1] = m_new
            hn = hout * lax.rsqrt(jnp.mean(hout * hout, axis=-1, keepdims=True) + EPS) * gh_ref[:, hs]
            out_ref[r0:r1, hs] = (ogate[r0:r1, hs] * hn).astype(bf16)


def _mlstm(pb, conv_w, bias_if, g_head, tm=256):
    s = pb.shape[0]
    return pl.pallas_call(
        functools.partial(_mlstm_kernel, tm=tm),
        grid=(s // tm,),
        in_specs=[pl.BlockSpec((tm, 2 * M_WIDTH), lambda i: (i, B_MQ // (2 * M_WIDTH))),
                  pl.BlockSpec((tm, M_WIDTH), lambda i: (i, B_MV // M_WIDTH)),
                  pl.BlockSpec((tm, M_WIDTH), lambda i: (i, B_MO // M_WIDTH)),
                  pl.BlockSpec((tm, LANE), lambda i: (i, B_IF // LANE)),
                  _const_spec((M_CONV, 2 * M_WIDTH)),
                  _const_spec((1, LANE)),
                  _const_spec((1, M_WIDTH))],
        out_specs=pl.BlockSpec((tm, M_WIDTH), lambda i: (i, 0)),
        out_shape=jax.ShapeDtypeStruct((s, M_WIDTH), bf16),
        scratch_shapes=[pltpu.VMEM((tm + SUBLANE, 2 * M_WIDTH), f32),
                        pltpu.VMEM((M_HEADS, M_DH, 2 * M_DH), f32),
                        pltpu.VMEM((SUBLANE, LANE), f32)],
        compiler_params=_params("arbitrary"),
        name="mlstm",
    )(pb, pb, pb, pb, conv_w, bias_if, g_head)


def _memkv_kernel(mem_ref, g_ref, wk_ref, wv_ref, k_ref, v_ref):
    hm = _rms(mem_ref[...], g_ref[...]).astype(bf16)
    k_ref[...] = _dot(hm, wk_ref[...]).astype(bf16)
    v_ref[...] = _dot(hm, wv_ref[...]).astype(bf16)


def _memkv(mem, g, wk, wv):
    m = mem.shape[0]
    return pl.pallas_call(
        _memkv_kernel,
        out_shape=[jax.ShapeDtypeStruct((m, D_MODEL), bf16)] * 2,
        compiler_params=pltpu.CompilerParams(vmem_limit_bytes=VMEM_LIMIT),
        name="mem_kv",
    )(mem, g, wk, wv)


def _mix_mem_kernel(x_ref, nsa_ref, ml_ref, won_ref, wom_ref, gq_ref, wq_ref, kt_ref, v_ref, wo_ref, o_ref):
    x1 = x_ref[...] + _dot(nsa_ref[...], won_ref[...]) + _dot(ml_ref[...], wom_ref[...])
    hq = _rms(x1, gq_ref[...]).astype(bf16)
    q = (_dot(hq, wq_ref[...]) * (MEM_DH ** -0.5)).astype(bf16)
    outs = []
    for h in range(MEM_HEADS):
        hs = slice(h * MEM_DH, (h + 1) * MEM_DH)
        s = _dot(q[:, hs], kt_ref[hs, :])
        e = jnp.exp(s - jnp.max(s, axis=-1, keepdims=True))
        p = e / jnp.sum(e, axis=-1, keepdims=True)
        outs.append(_dot(p.astype(bf16), v_ref[:, hs]).astype(bf16))
    o_ref[...] = x1 + _dot(jnp.concatenate(outs, axis=1), wo_ref[...])


def _mix_mem(x, nsa_n, ml, w_out_nsa, w_out_m, g_q, w_q, k_t, v, w_o, tm=512):
    s = x.shape[0]
    m = v.shape[0]
    return pl.pallas_call(
        _mix_mem_kernel,
        grid=(s // tm,),
        in_specs=[pl.BlockSpec((tm, D_MODEL), lambda i: (i, 0)),
                  pl.BlockSpec((tm, NSA_HEADS * LANE), lambda i: (i, 0)),
                  pl.BlockSpec((tm, M_WIDTH), lambda i: (i, 0)),
                  _const_spec((NSA_HEADS * LANE, D_MODEL)),
                  _const_spec((M_WIDTH, D_MODEL)),
                  _const_spec((1, D_MODEL)),
                  _const_spec((D_MODEL, D_MODEL)),
                  _const_spec((D_MODEL, m)),
                  _const_spec((m, D_MODEL)),
                  _const_spec((D_MODEL, D_MODEL))],
        out_specs=pl.BlockSpec((tm, D_MODEL), lambda i: (i, 0)),
        out_shape=jax.ShapeDtypeStruct((s, D_MODEL), f32),
        compiler_params=_params("parallel"),
        name="mix_mem",
    )(x, nsa_n, ml, w_out_nsa, w_out_m, g_q, w_q, k_t, v, w_o)


def _ffn_kernel(x_ref, g_ref, wup_ref, cw_ref, wdn_ref, gf_ref, o_ref, ubuf, *, tm, cw):
    @pl.when(pl.program_id(0) == 0)
    def _():
        ubuf[0:SUBLANE, :] = jnp.zeros((SUBLANE, 2 * D_FF), f32)

    x = x_ref[...]
    hf = _rms(x, g_ref[...]).astype(bf16)
    lo = SUBLANE - (F_CONV - 1)

    def conv_cols(c0):
        u = _dot(hf, wup_ref[:, c0:c0 + cw])
        ubuf[SUBLANE:SUBLANE + tm, c0:c0 + cw] = u
        y = cw_ref[F_CONV - 1:F_CONV, c0:c0 + cw] * u
        for k in range(F_CONV - 1):
            y = y + cw_ref[k:k + 1, c0:c0 + cw] * ubuf[lo + k:lo + k + tm, c0:c0 + cw]
        return y

    acc = x
    for c0 in range(0, D_FF, cw):
        gated = _gelu(conv_cols(c0)) * conv_cols(D_FF + c0)
        acc = acc + _dot(gated.astype(bf16), wdn_ref[c0:c0 + cw, :])
    ubuf[0:SUBLANE, :] = ubuf[tm:tm + SUBLANE, :]
    o_ref[...] = _rms(acc, gf_ref[...])


def _ffn(x, g, w_up, conv_w, w_down, g_final, tm=256, cw=256):
    s = x.shape[0]
    return pl.pallas_call(
        functools.partial(_ffn_kernel, tm=tm, cw=cw),
        grid=(s // tm,),
        in_specs=[pl.BlockSpec((tm, D_MODEL), lambda i: (i, 0)),
                  _const_spec((1, D_MODEL)),
                  _const_spec((D_MODEL, 2 * D_FF)),
                  _const_spec((F_CONV, 2 * D_FF)),
                  _const_spec((D_FF, D_MODEL)),
                  _const_spec((1, D_MODEL))],
        out_specs=pl.BlockSpec((tm, D_MODEL), lambda i: (i, 0)),
        out_shape=jax.ShapeDtypeStruct((s, D_MODEL), f32),
        scratch_shapes=[pltpu.VMEM((tm + SUBLANE, 2 * D_FF), f32)],
        compiler_params=_params("arbitrary"),
        name="ffn",
    )(x, g, w_up, conv_w, w_down, g_final)


def _pad_cols(a, n):
    return jnp.pad(a, ((0, 0), (0, n - a.shape[1])))


def _pad_heads(a, heads, dh):
    r = a.shape[0]
    return jnp.pad(a.reshape(r, heads, dh), ((0, 0), (0, 0), (0, LANE - dh))).reshape(r, heads * LANE)


def _build_w_in(w):
    o = 0
    parts = {}
    sizes = (("q", 512), ("kc", 128), ("vc", 128), ("ks", 128), ("vs", 128), ("kw", 128), ("vw", 128),
             ("gt", 24), ("mq", 512), ("mk", 512), ("mv", 512), ("mo", 512), ("mi", 4), ("mf", 4))
    for name, n in sizes:
        parts[name] = w[:, o:o + n]
        o += n
    gt = parts["gt"].reshape(D_MODEL, NSA_GROUPS, NSA_HPG * 3)
    cols = [_pad_heads(parts["q"], NSA_HEADS, NSA_DH), parts["ks"], parts["vs"], parts["kw"], parts["vw"],
            parts["mq"], parts["mk"], parts["mv"], parts["mo"], parts["kc"], parts["vc"],
            _pad_cols(gt[:, 0], LANE), _pad_cols(gt[:, 1], LANE),
            _pad_cols(jnp.concatenate([parts["mi"], parts["mf"]], axis=1), LANE)]
    return jnp.concatenate(cols, axis=1).astype(bf16)


def _group_t(a, s):
    return a.reshape(s, NSA_GROUPS, NSA_DH).transpose(1, 2, 0)


def _values_aug(a, s):
    v = a.reshape(s, NSA_GROUPS, NSA_DH).transpose(1, 0, 2)
    ones = jnp.ones((NSA_GROUPS, s, 1), a.dtype)
    zeros = jnp.zeros((NSA_GROUPS, s, LANE - NSA_DH - 1), a.dtype)
    return jnp.concatenate([v, ones, zeros], axis=2)


def kernel(x, mem, g_mix, w_in, cmp_pos_k, cmp_w1_k, cmp_w2_k, cmp_pos_v, cmp_w1_v, cmp_w2_v, mlstm_conv_w, mlstm_b_i, mlstm_b_f, g_head_nsa, g_head_mlstm, w_out, g_mem_q, g_mem_kv, w_mem_q, w_mem_k, w_mem_v, w_mem_o, g_ffn, w_up, ffn_conv_w, w_down, g_final):
    b, s, _ = x.shape
    assert b == 1 and g_mix.shape[0] == 1
    sel_tk = 1024
    assert s % 8192 == 0 and s >= sel_tk + WINDOW
    l = 0
    xs = x[0]
    row = lambda a: a.reshape(1, -1)

    pa, pb = _inproj(xs, row(g_mix[l]), _build_w_in(w_in[l]))

    n16 = s // CMP_STRIDE
    hflat = (pb[:, B_KC:B_G0].reshape(n16, CMP_STRIDE, 2, NSA_GROUPS, NSA_DH)
             .transpose(2, 3, 0, 1, 4).reshape(2, NSA_GROUPS, n16, CMP_STRIDE * NSA_DH))
    pos = jnp.stack([cmp_pos_k[l].reshape(1, -1), cmp_pos_v[l].reshape(1, -1)])
    w1 = jnp.stack([cmp_w1_k[l], cmp_w1_v[l]]).astype(bf16)
    w2 = jnp.stack([cmp_w2_k[l], cmp_w2_v[l]]).astype(bf16)
    cmp = _compress(hflat, pos, w1, w2)
    kct = jnp.pad(cmp[0].transpose(0, 2, 1), ((0, 0), (0, LANE - NSA_DH), (0, 0))).astype(bf16)
    vc = jnp.pad(cmp[1], ((0, 0), (0, 0), (0, LANE - NSA_DH))).astype(bf16)

    n_sel = s // SEL_BLOCK
    cstart = jnp.arange(n16)[:, None] * CMP_STRIDE
    sstart = jnp.arange(n_sel)[None, :] * SEL_BLOCK
    overlap = ((cstart < sstart + SEL_BLOCK) & (cstart + CMP_LEN > sstart)).astype(bf16)
    oc, sel = _cmp_attention(pa, kct, vc, overlap)

    nblk = sel_tk // SEL_BLOCK
    kidx = (jnp.arange(s)[None, :] // SEL_BLOCK) % nblk
    ind = (kidx == jnp.arange(nblk)[:, None]).astype(bf16)
    ind = jnp.broadcast_to(ind[None], (NSA_GROUPS, nblk, s))
    zpad = lambda n: jnp.zeros((NSA_GROUPS, n, s), bf16)
    kst = jnp.concatenate([_group_t(pa[:, A_KS:A_VS], s), ind, zpad(LANE - NSA_DH - nblk)], axis=1)
    kwt = jnp.concatenate([_group_t(pa[:, A_KW:A_VW], s), zpad(LANE - NSA_DH)], axis=1)
    vs = _values_aug(pa[:, A_VS:A_KW], s)
    vw = _values_aug(pa[:, A_VW:NA], s)
    gh_pad = _pad_heads(row(g_head_nsa[l]), NSA_HEADS, NSA_DH)
    nsa_n = _selwin_attention(pa, pb, oc, sel, gh_pad, kst, vs, kwt, vw, tk=sel_tk)

    bias_if = _pad_cols(jnp.concatenate([row(mlstm_b_i[l]), row(mlstm_b_f[l])], axis=1), LANE)
    ml = _mlstm(pb, mlstm_conv_w[l], bias_if, row(g_head_mlstm[l]))

    w_out_nsa = jnp.pad(w_out[l][:NSA_HEADS * NSA_DH].reshape(NSA_HEADS, NSA_DH, D_MODEL),
                        ((0, 0), (0, LANE - NSA_DH), (0, 0))).reshape(NSA_HEADS * LANE, D_MODEL).astype(bf16)
    w_out_m = w_out[l][NSA_HEADS * NSA_DH:].astype(bf16)
    k_mem, v_mem = _memkv(mem[0], row(g_mem_kv[l]), w_mem_k[l].astype(bf16), w_mem_v[l].astype(bf16))
    x2 = _mix_mem(xs, nsa_n, ml, w_out_nsa, w_out_m, row(g_mem_q[l]), w_mem_q[l].astype(bf16),
                  k_mem.T, v_mem, w_mem_o[l].astype(bf16))

    out = _ffn(x2, row(g_ffn[l]), w_up[l].astype(bf16), ffn_conv_w[l], w_down[l].astype(bf16), row(g_final))
    return out[None]
```

```python
import functools

import jax
import jax.numpy as jnp
from jax import lax
from jax.experimental import pallas as pl
from jax.experimental.pallas import tpu as pltpu

f32 = jnp.float32
bf16 = jnp.bfloat16

D_MODEL = 1024
NSA_HEADS = 8
NSA_GROUPS = 2
NSA_HPG = NSA_HEADS // NSA_GROUPS
NSA_DH = 64
CMP_LEN = 32
CMP_STRIDE = 16
SEL_BLOCK = 64
SEL_TOPK = 16
WINDOW = 512
FORCE_BONUS = 1.0e4
M_HEADS = 4
M_DH = 128
M_WIDTH = M_HEADS * M_DH
M_CONV = 4
M_CHUNK = 64
MEM_HEADS = 4
MEM_DH = D_MODEL // MEM_HEADS
D_FF = 2816
F_CONV = 3
EPS = 1e-6
NSA_QSCALE = NSA_DH ** -0.5 * 1.4426950408889634

LANE = 128
SUBLANE = 8
NEG = -1e30
VMEM_LIMIT = 56 * 1024 * 1024

A_Q, A_KS, A_VS, A_KW, A_VW, NA = 0, 1024, 1152, 1280, 1408, 1536
B_MQ, B_MV, B_MO, B_KC, B_VC, B_G0, B_IF, NB = 0, 1024, 1536, 2048, 2176, 2304, 2560, 2688


def _dot(a, b):
    return jnp.dot(a, b, preferred_element_type=f32)


def _rms(x, g):
    return x * lax.rsqrt(jnp.mean(x * x, axis=-1, keepdims=True) + EPS) * g


def _gelu(x):
    return 0.5 * x * (1.0 + lax.erf(x * 0.7071067811865476))


def _params(*sem):
    return pltpu.CompilerParams(dimension_semantics=sem, vmem_limit_bytes=VMEM_LIMIT)


def _const_spec(shape):
    nd = len(shape)
    return pl.BlockSpec(shape, lambda *_: (0,) * nd, pipeline_mode=pl.Buffered(1))


def _split_keys_t(ref, k, extra=None):
    kt = k.T
    n_extra = 0 if extra is None else extra.shape[0]
    for g in range(NSA_GROUPS):
        ref[g, 0:NSA_DH, :] = kt[g * NSA_DH:(g + 1) * NSA_DH].astype(bf16)
        if extra is not None:
            ref[g, NSA_DH:NSA_DH + n_extra, :] = extra
        ref[g, NSA_DH + n_extra:, :] = jnp.zeros((LANE - NSA_DH - n_extra, kt.shape[1]), bf16)


def _split_values(ref, v, ones_col):
    lane = lax.broadcasted_iota(jnp.int32, v.shape, 1)
    fill = jnp.where(lane == NSA_DH, 1.0, 0.0) if ones_col else 0.0
    for g in range(NSA_GROUPS):
        vg = v if g == 0 else pltpu.roll(v, LANE - g * NSA_DH, 1)
        ref[g] = jnp.where(lane < NSA_DH, vg, fill).astype(bf16)


def _inproj_kernel(x_ref, g_ref, w_ref, q_ref, kst_ref, vs_ref, kwt_ref, vw_ref, pb_ref, *, tm, nblk):
    h = _rms(x_ref[...], g_ref[...]).astype(bf16)
    q_ref[...] = (_dot(h, w_ref[:, A_Q:A_KS]) * NSA_QSCALE).astype(bf16)
    kv = _dot(h, w_ref[:, A_KS:NA])
    key = pl.program_id(0) * tm + lax.broadcasted_iota(jnp.int32, (nblk, tm), 1)
    rowj = lax.broadcasted_iota(jnp.int32, (nblk, tm), 0)
    ind = ((lax.shift_right_logical(key, 6) & (nblk - 1)) == rowj).astype(bf16)
    _split_keys_t(kst_ref, kv[:, 0:LANE], ind)
    _split_values(vs_ref, kv[:, LANE:2 * LANE], True)
    _split_keys_t(kwt_ref, kv[:, 2 * LANE:3 * LANE])
    _split_values(vw_ref, kv[:, 3 * LANE:4 * LANE], True)
    for c0 in range(0, NB, 512):
        c1 = min(c0 + 512, NB)
        pb_ref[:, c0:c1] = _dot(h, w_ref[:, NA + c0:NA + c1])


def _inproj(x, g, w, nblk, tm=512):
    s = x.shape[0]
    assert nblk & (nblk - 1) == 0 and SEL_BLOCK == 64
    kspec = pl.BlockSpec((NSA_GROUPS, LANE, tm), lambda i: (0, 0, i))
    vspec = pl.BlockSpec((NSA_GROUPS, tm, LANE), lambda i: (0, i, 0))
    kshape = jax.ShapeDtypeStruct((NSA_GROUPS, LANE, s), bf16)
    vshape = jax.ShapeDtypeStruct((NSA_GROUPS, s, LANE), bf16)
    return pl.pallas_call(
        functools.partial(_inproj_kernel, tm=tm, nblk=nblk),
        grid=(s // tm,),
        in_specs=[pl.BlockSpec((tm, D_MODEL), lambda i: (i, 0)),
                  _const_spec((1, D_MODEL)),
                  _const_spec((D_MODEL, NA + NB))],
        out_specs=[pl.BlockSpec((tm, NSA_HEADS * LANE), lambda i: (i, 0)), kspec, vspec, kspec, vspec,
                   pl.BlockSpec((tm, NB), lambda i: (i, 0))],
        out_shape=[jax.ShapeDtypeStruct((s, NSA_HEADS * LANE), bf16), kshape, vshape, kshape, vshape,
                   jax.ShapeDtypeStruct((s, NB), f32)],
        compiler_params=_params("parallel"),
        name="inproj",
    )(x, g, w)


def _compress_kernel(kc_ref, vc_ref, pos_ref, w1_ref, w2_ref, kct_ref, vcp_ref):
    n = kc_ref.shape[0] // CMP_STRIDE

    def mlp(x_ref, a):
        top = bot = None
        for l in range(CMP_STRIDE):
            xl = x_ref[pl.ds(l, n, stride=CMP_STRIDE), :]
            t = _dot((xl + pos_ref[a, l:l + 1, :]).astype(bf16), w1_ref[a, l])
            b = _dot((xl + pos_ref[a, CMP_STRIDE + l:CMP_STRIDE + l + 1, :]).astype(bf16), w1_ref[a, CMP_STRIDE + l])
            top = t if top is None else top + t
            bot = b if bot is None else bot + b
        pre = top + pltpu.roll(bot, n - 1, 0)
        return _dot(_gelu(pre).astype(bf16), w2_ref[a])

    _split_keys_t(kct_ref, mlp(kc_ref, 0))
    _split_values(vcp_ref, mlp(vc_ref, 1), False)


def _compress(pb, pos, w1, w2):
    s = pb.shape[0]
    n = s // CMP_STRIDE
    return pl.pallas_call(
        _compress_kernel,
        grid=(1,),
        in_specs=[pl.BlockSpec((s, LANE), lambda i: (0, B_KC // LANE)),
                  pl.BlockSpec((s, LANE), lambda i: (0, B_VC // LANE)),
                  _const_spec(pos.shape), _const_spec(w1.shape), _const_spec(w2.shape)],
        out_specs=[pl.BlockSpec((NSA_GROUPS, LANE, n), lambda i: (0, 0, 0)),
                   pl.BlockSpec((NSA_GROUPS, n, LANE), lambda i: (0, 0, 0))],
        out_shape=[jax.ShapeDtypeStruct((NSA_GROUPS, LANE, n), bf16),
                   jax.ShapeDtypeStruct((NSA_GROUPS, n, LANE), bf16)],
        compiler_params=_params("arbitrary"),
        name="compress",
    )(pb, pb, pos, w1, w2)


def _cmp_kernel(q_ref, kct_ref, vc_ref, ov_ref, oc_ref, sel_ref, *, tq):
    ncmp = kct_ref.shape[2]
    nsel = ov_ref.shape[1]
    t0 = pl.program_id(0) * tq
    tpos = t0 + lax.broadcasted_iota(jnp.int32, (tq, 1), 0)
    cend = lax.broadcasted_iota(jnp.int32, (1, ncmp), 1) * CMP_STRIDE + (CMP_LEN - 1)
    vis = cend <= tpos
    any_vis = tpos >= CMP_LEN - 1
    cur = lax.shift_right_logical(tpos, 6)
    blk = lax.broadcasted_iota(jnp.int32, (1, nsel), 1)
    forced = (blk == 0) | (blk == cur) | (blk == cur - 1)
    scores = []
    for g in range(NSA_GROUPS):
        psum = None
        for hh in range(NSA_HPG):
            c0 = (g * NSA_HPG + hh) * LANE
            s = jnp.where(vis, _dot(q_ref[:, c0:c0 + LANE], kct_ref[g]), NEG)
            e = jnp.exp2(s - jnp.max(s, axis=-1, keepdims=True))
            l = jnp.sum(e, axis=-1, keepdims=True)
            p = e * jnp.where(any_vis, 1.0 / l, 0.0)
            oc_ref[:, c0:c0 + LANE] = _dot(p.astype(bf16), vc_ref[g])
            psum = p if psum is None else psum + p
        imp = _dot(psum.astype(bf16), ov_ref[...])
        scores.append(jnp.where(forced, -jnp.inf, jnp.where(blk <= cur, imp, -FORCE_BONUS)))
    score = jnp.concatenate(scores, axis=0).T
    blkt = lax.broadcasted_iota(jnp.int32, score.shape, 0).astype(f32)
    for _ in range(SEL_TOPK - 3):
        mx = jnp.max(score, axis=0, keepdims=True)
        j = jnp.min(jnp.where(score == mx, blkt, float(nsel)), axis=0, keepdims=True)
        score = jnp.where(blkt == j, -jnp.inf, score)
    selm = (score == -jnp.inf).astype(f32).T
    for g in range(NSA_GROUPS):
        sel_ref[g] = selm[g * tq:(g + 1) * tq]


def _cmp_attention(pa, kct, vc, ov, tq=512):
    s = pa.shape[0]
    ncmp, nsel = ov.shape
    return pl.pallas_call(
        functools.partial(_cmp_kernel, tq=tq),
        grid=(s // tq,),
        in_specs=[pl.BlockSpec((tq, NSA_HEADS * LANE), lambda i: (i, 0)),
                  _const_spec((NSA_GROUPS, LANE, ncmp)),
                  _const_spec((NSA_GROUPS, ncmp, LANE)),
                  _const_spec((ncmp, nsel))],
        out_specs=[pl.BlockSpec((tq, NSA_HEADS * LANE), lambda i: (i, 0)),
                   pl.BlockSpec((NSA_GROUPS, tq, nsel), lambda i: (0, i, 0))],
        out_shape=[jax.ShapeDtypeStruct((s, NSA_HEADS * LANE), f32),
                   jax.ShapeDtypeStruct((NSA_GROUPS, s, nsel), f32)],
        compiler_params=_params("parallel"),
        name="cmp_attn",
    )(pa, kct, vc, ov)


def _selwin_kernel(q_ref, oc_ref, sel_ref, gate_ref, gh_ref, kst_ref, vs_ref, kwt_ref, vw_ref, out_ref, *, tq, tk):
    nsel = sel_ref.shape[2]
    nblk = tk // SEL_BLOCK
    t0 = pl.program_id(1) * tq
    tpos = t0 + lax.broadcasted_iota(jnp.int32, (tq, 1), 0)
    q4 = [q_ref[:, hh * LANE:(hh + 1) * LANE] for hh in range(NSA_HPG)]
    sel = sel_ref[0]
    lane = lax.broadcasted_iota(jnp.int32, (tq, LANE), 1)
    bias_lanes = (lane >= NSA_DH) & (lane < NSA_DH + nblk)

    def tile(kt, carry, causal):
        st = pl.multiple_of(kt * tk, tk)
        kblk = kst_ref[0, :, pl.ds(st, tk)]
        vblk = vs_ref[0, pl.ds(st, tk), :]
        sh = lax.rem(NSA_DH + nsel - kt * nblk, jnp.int32(nsel))
        bias = ((pltpu.roll(sel, sh, 1)[:, :LANE] - 1.0) * -NEG).astype(bf16)
        if causal:
            keep = st + lax.broadcasted_iota(jnp.int32, (1, tk), 1) <= tpos
        out = []
        for hh in range(NSA_HPG):
            m, acc = carry[hh]
            s = _dot(jnp.where(bias_lanes, bias, q4[hh]), kblk)
            if causal:
                s = jnp.where(keep, s, NEG)
            mn = jnp.maximum(m, jnp.max(s, axis=-1, keepdims=True))
            p = jnp.exp2(s - mn)
            out.append((mn, jnp.exp2(m - mn) * acc + _dot(p.astype(bf16), vblk)))
        return tuple(out)

    last = (t0 + tq - 1) // tk
    init = tuple((jnp.full((tq, 1), NEG, f32), jnp.zeros((tq, LANE), f32)) for _ in range(NSA_HPG))
    carry = lax.fori_loop(0, last, lambda kt, c: tile(kt, c, False), init)
    carry = tile(last, carry, True)

    wlen = tq + WINDOW
    ws = pl.multiple_of(jnp.maximum(t0 - WINDOW, 0), LANE)
    kwin = kwt_ref[0, :, pl.ds(ws, wlen)]
    vwin = vw_ref[0, pl.ds(ws, wlen), :]
    kpos = ws + lax.broadcasted_iota(jnp.int32, (1, wlen), 1)
    wmask = (kpos <= tpos) & (kpos > tpos - WINDOW)

    gates = jax.nn.sigmoid(gate_ref[...])
    valid = lane < NSA_DH
    for hh in range(NSA_HPG):
        sw = jnp.where(wmask, _dot(q4[hh], kwin), NEG)
        pw = jnp.exp2(sw - jnp.max(sw, axis=-1, keepdims=True))
        acc_w = _dot(pw.astype(bf16), vwin)
        acc_s = carry[hh][1]
        o_s = acc_s / acc_s[:, NSA_DH:NSA_DH + 1]
        o_w = acc_w / acc_w[:, NSA_DH:NSA_DH + 1]
        o_c = oc_ref[:, hh * LANE:(hh + 1) * LANE]
        o = (gates[:, 3 * hh:3 * hh + 1] * o_c + gates[:, 3 * hh + 1:3 * hh + 2] * o_s
             + gates[:, 3 * hh + 2:3 * hh + 3] * o_w)
        o = jnp.where(valid, o, 0.0)
        ms = jnp.sum(o * o, axis=-1, keepdims=True) * (1.0 / NSA_DH)
        y = o * lax.rsqrt(ms + EPS) * gh_ref[:, hh * LANE:(hh + 1) * LANE]
        out_ref[:, hh * LANE:(hh + 1) * LANE] = y.astype(bf16)


def _selwin_attention(pa, pb, oc, sel, gh_pad, kst, vs, kwt, vw, tq=512, tk=2048):
    s = pa.shape[0]
    nsel = sel.shape[2]
    gw = NSA_HPG * LANE
    kspec = pl.BlockSpec((1, LANE, s), lambda g, i: (g, 0, 0), pipeline_mode=pl.Buffered(1))
    vspec = pl.BlockSpec((1, s, LANE), lambda g, i: (g, 0, 0), pipeline_mode=pl.Buffered(1))
    return pl.pallas_call(
        functools.partial(_selwin_kernel, tq=tq, tk=tk),
        grid=(NSA_GROUPS, s // tq),
        in_specs=[pl.BlockSpec((tq, gw), lambda g, i: (i, g)),
                  pl.BlockSpec((tq, gw), lambda g, i: (i, g)),
                  pl.BlockSpec((1, tq, nsel), lambda g, i: (g, i, 0)),
                  pl.BlockSpec((tq, LANE), lambda g, i: (i, B_G0 // LANE + g)),
                  pl.BlockSpec((1, gw), lambda g, i: (0, g)),
                  kspec, vspec, kspec, vspec],
        out_specs=pl.BlockSpec((tq, gw), lambda g, i: (i, g)),
        out_shape=jax.ShapeDtypeStruct((s, NSA_HEADS * LANE), bf16),
        compiler_params=_params("parallel", "parallel"),
        name="selwin_attn",
    )(pa, oc, sel, pb, gh_pad, kst, vs, kwt, vw)


def _mlstm_kernel(qk_ref, v_ref, og_ref, if_ref, cw_ref, bias_ref, gh_ref, out_ref, buf, cst, mst, *, tm):
    @pl.when(pl.program_id(0) == 0)
    def _():
        buf[0:SUBLANE, :] = jnp.zeros((SUBLANE, 2 * M_WIDTH), f32)
        cst[...] = jnp.zeros(cst.shape, f32)
        mst[...] = jnp.zeros(mst.shape, f32)

    qk = qk_ref[...]
    buf[SUBLANE:SUBLANE + tm, :] = qk
    conv = cw_ref[M_CONV - 1:M_CONV, :] * qk
    for k in range(M_CONV - 1):
        off = SUBLANE - (M_CONV - 1) + k
        conv = conv + cw_ref[k:k + 1, :] * buf[off:off + tm, :]
    buf[0:SUBLANE, :] = qk[tm - SUBLANE:tm]
    qkc = conv * jax.nn.sigmoid(conv)
    qs = (qkc[:, :M_WIDTH] * (M_DH ** -0.5)).astype(bf16)
    ks = qkc[:, M_WIDTH:]
    vb = v_ref[...].astype(bf16)
    ogate = jax.nn.sigmoid(og_ref[...])

    pre = if_ref[...] + bias_ref[...]
    lane = lax.broadcasted_iota(jnp.int32, (M_CHUNK, LANE), 1)
    rowi = lax.broadcasted_iota(jnp.int32, (M_CHUNK, LANE), 0)
    logf = jnp.minimum(pre, 0.0) - jnp.log1p(jnp.exp(-jnp.abs(pre)))
    tri = (lax.broadcasted_iota(jnp.int32, (M_CHUNK, M_CHUNK), 0)
           >= lax.broadcasted_iota(jnp.int32, (M_CHUNK, M_CHUNK), 1))
    ones_col = (lax.broadcasted_iota(jnp.int32, (M_CHUNK, LANE), 1) == 0).astype(bf16)

    for c in range(tm // M_CHUNK):
        r0, r1 = c * M_CHUNK, (c + 1) * M_CHUNK
        cum = logf[r0:r1]
        sft = 1
        while sft < M_CHUNK:
            cum = cum + jnp.where(rowi >= sft, pltpu.roll(cum, sft, 0), 0.0)
            sft *= 2
        comb = jnp.where(lane < M_HEADS, pre[r0:r1], cum)
        comb_t = comb.T
        for h in range(M_HEADS):
            hs = slice(h * M_DH, (h + 1) * M_DH)
            bcol = comb[:, M_HEADS + h:M_HEADS + h + 1]
            icol = comb[:, h:h + 1]
            brow = comb_t[M_HEADS + h:M_HEADS + h + 1, :]
            irow = comb_t[h:h + 1, :]
            mprev = mst[0:1, h:h + 1]
            dmat = jnp.where(tri, bcol - brow + irow, -jnp.inf)
            m_inter = bcol + mprev
            m_t = jnp.maximum(jnp.max(dmat, axis=-1, keepdims=True), m_inter)
            a_inter = jnp.exp(m_inter - m_t)
            qh = qs[r0:r1, hs]
            kh = ks[r0:r1, hs]
            v_aug = jnp.concatenate([vb[r0:r1, hs], ones_col], axis=1)
            qk_s = lax.dot_general(qh, kh.astype(bf16), (((1,), (1,)), ((), ())), preferred_element_type=f32)
            sc = qk_s * jnp.exp(dmat - m_t)
            tot = a_inter * _dot(qh, cst[h].astype(bf16)) + _dot(sc.astype(bf16), v_aug)
            den = tot[:, M_DH:M_DH + 1]
            hout = tot[:, :M_DH] / jnp.maximum(jnp.abs(den), jnp.exp(-m_t))
            b_last = bcol[M_CHUNK - 1:M_CHUNK, :]
            g_s = b_last - bcol + icol
            m_new = jnp.maximum(b_last + mprev, jnp.max(g_s, axis=0, keepdims=True))
            w_s = jnp.exp(g_s - m_new)
            decay = jnp.exp(b_last + mprev - m_new)
            upd = lax.dot_general((kh * w_s).astype(bf16), v_aug, (((0,), (0,)), ((), ())),
                                  preferred_element_type=f32)
            cst[h] = decay * cst[h] + upd
            mst[0:1, h:h + 1] = m_new
            hn = hout * lax.rsqrt(jnp.mean(hout * hout, axis=-1, keepdims=True) + EPS) * gh_ref[:, hs]
            out_ref[r0:r1, hs] = (ogate[r0:r1, hs] * hn).astype(bf16)


def _mlstm(pb, conv_w, bias_if, g_head, tm=256):
    s = pb.shape[0]
    return pl.pallas_call(
        functools.partial(_mlstm_kernel, tm=tm),
        grid=(s // tm,),
        in_specs=[pl.BlockSpec((tm, 2 * M_WIDTH), lambda i: (i, B_MQ // (2 * M_WIDTH))),
                  pl.BlockSpec((tm, M_WIDTH), lambda i: (i, B_MV // M_WIDTH)),
                  pl.BlockSpec((tm, M_WIDTH), lambda i: (i, B_MO // M_WIDTH)),
                  pl.BlockSpec((tm, LANE), lambda i: (i, B_IF // LANE)),
                  _const_spec((M_CONV, 2 * M_WIDTH)),
                  _const_spec((1, LANE)),
                  _const_spec((1, M_WIDTH))],
        out_specs=pl.BlockSpec((tm, M_WIDTH), lambda i: (i, 0)),
        out_shape=jax.ShapeDtypeStruct((s, M_WIDTH), bf16),
        scratch_shapes=[pltpu.VMEM((tm + SUBLANE, 2 * M_WIDTH), f32),
                        pltpu.VMEM((M_HEADS, M_DH, 2 * M_DH), f32),
                        pltpu.VMEM((SUBLANE, LANE), f32)],
        compiler_params=_params("arbitrary"),
        name="mlstm",
    )(pb, pb, pb, pb, conv_w, bias_if, g_head)


def _memkv_kernel(mem_ref, g_ref, wk_ref, wv_ref, k_ref, v_ref):
    hm = _rms(mem_ref[...], g_ref[...]).astype(bf16)
    k_ref[...] = _dot(hm, wk_ref[...]).astype(bf16)
    v_ref[...] = _dot(hm, wv_ref[...]).astype(bf16)


def _memkv(mem, g, wk, wv):
    m = mem.shape[0]
    return pl.pallas_call(
        _memkv_kernel,
        out_shape=[jax.ShapeDtypeStruct((m, D_MODEL), bf16)] * 2,
        compiler_params=pltpu.CompilerParams(vmem_limit_bytes=VMEM_LIMIT),
        name="mem_kv",
    )(mem, g, wk, wv)


def _mix_mem_kernel(x_ref, nsa_ref, ml_ref, won_ref, wom_ref, gq_ref, wq_ref, kt_ref, v_ref, wo_ref, o_ref):
    x1 = x_ref[...] + _dot(nsa_ref[...], won_ref[...]) + _dot(ml_ref[...], wom_ref[...])
    hq = _rms(x1, gq_ref[...]).astype(bf16)
    q = (_dot(hq, wq_ref[...]) * (MEM_DH ** -0.5)).astype(bf16)
    outs = []
    for h in range(MEM_HEADS):
        hs = slice(h * MEM_DH, (h + 1) * MEM_DH)
        s = _dot(q[:, hs], kt_ref[hs, :])
        e = jnp.exp(s - jnp.max(s, axis=-1, keepdims=True))
        p = e / jnp.sum(e, axis=-1, keepdims=True)
        outs.append(_dot(p.astype(bf16), v_ref[:, hs]).astype(bf16))
    o_ref[...] = x1 + _dot(jnp.concatenate(outs, axis=1), wo_ref[...])


def _mix_mem(x, nsa_n, ml, w_out_nsa, w_out_m, g_q, w_q, k_t, v, w_o, tm=512):
    s = x.shape[0]
    m = v.shape[0]
    return pl.pallas_call(
        _mix_mem_kernel,
        grid=(s // tm,),
        in_specs=[pl.BlockSpec((tm, D_MODEL), lambda i: (i, 0)),
                  pl.BlockSpec((tm, NSA_HEADS * LANE), lambda i: (i, 0)),
                  pl.BlockSpec((tm, M_WIDTH), lambda i: (i, 0)),
                  _const_spec((NSA_HEADS * LANE, D_MODEL)),
                  _const_spec((M_WIDTH, D_MODEL)),
                  _const_spec((1, D_MODEL)),
                  _const_spec((D_MODEL, D_MODEL)),
                  _const_spec((D_MODEL, m)),
                  _const_spec((m, D_MODEL)),
                  _const_spec((D_MODEL, D_MODEL))],
        out_specs=pl.BlockSpec((tm, D_MODEL), lambda i: (i, 0)),
        out_shape=jax.ShapeDtypeStruct((s, D_MODEL), f32),
        compiler_params=_params("parallel"),
        name="mix_mem",
    )(x, nsa_n, ml, w_out_nsa, w_out_m, g_q, w_q, k_t, v, w_o)


def _ffn_kernel(x_ref, g_ref, wup_ref, cw_ref, wdn_ref, gf_ref, o_ref, ubuf, *, tm, cw):
    @pl.when(pl.program_id(0) == 0)
    def _():
        ubuf[0:SUBLANE, :] = jnp.zeros((SUBLANE, 2 * D_FF), f32)

    x = x_ref[...]
    hf = _rms(x, g_ref[...]).astype(bf16)
    lo = SUBLANE - (F_CONV - 1)

    def conv_cols(c0):
        u = _dot(hf, wup_ref[:, c0:c0 + cw])
        ubuf[SUBLANE:SUBLANE + tm, c0:c0 + cw] = u
        y = cw_ref[F_CONV - 1:F_CONV, c0:c0 + cw] * u
        for k in range(F_CONV - 1):
            y = y + cw_ref[k:k + 1, c0:c0 + cw] * ubuf[lo + k:lo + k + tm, c0:c0 + cw]
        return y

    acc = x
    for c0 in range(0, D_FF, cw):
        gated = _gelu(conv_cols(c0)) * conv_cols(D_FF + c0)
        acc = acc + _dot(gated.astype(bf16), wdn_ref[c0:c0 + cw, :])
    ubuf[0:SUBLANE, :] = ubuf[tm:tm + SUBLANE, :]
    o_ref[...] = _rms(acc, gf_ref[...])


def _ffn(x, g, w_up, conv_w, w_down, g_final, tm=512, cw=256):
    s = x.shape[0]
    return pl.pallas_call(
        functools.partial(_ffn_kernel, tm=tm, cw=cw),
        grid=(s // tm,),
        in_specs=[pl.BlockSpec((tm, D_MODEL), lambda i: (i, 0)),
                  _const_spec((1, D_MODEL)),
                  _const_spec((D_MODEL, 2 * D_FF)),
                  _const_spec((F_CONV, 2 * D_FF)),
                  _const_spec((D_FF, D_MODEL)),
                  _const_spec((1, D_MODEL))],
        out_specs=pl.BlockSpec((tm, D_MODEL), lambda i: (i, 0)),
        out_shape=jax.ShapeDtypeStruct((s, D_MODEL), f32),
        scratch_shapes=[pltpu.VMEM((tm + SUBLANE, 2 * D_FF), f32)],
        compiler_params=_params("arbitrary"),
        name="ffn",
    )(x, g, w_up, conv_w, w_down, g_final)


def _pad_cols(a, n):
    return jnp.pad(a, ((0, 0), (0, n - a.shape[1])))


def _pad_heads(a, heads, dh):
    r = a.shape[0]
    return jnp.pad(a.reshape(r, heads, dh), ((0, 0), (0, 0), (0, LANE - dh))).reshape(r, heads * LANE)


def _build_w_in(w):
    o = 0
    parts = {}
    sizes = (("q", 512), ("kc", 128), ("vc", 128), ("ks", 128), ("vs", 128), ("kw", 128), ("vw", 128),
             ("gt", 24), ("mq", 512), ("mk", 512), ("mv", 512), ("mo", 512), ("mi", 4), ("mf", 4))
    for name, n in sizes:
        parts[name] = w[:, o:o + n]
        o += n
    gt = parts["gt"].reshape(D_MODEL, NSA_GROUPS, NSA_HPG * 3)
    cols = [_pad_heads(parts["q"], NSA_HEADS, NSA_DH), parts["ks"], parts["vs"], parts["kw"], parts["vw"],
            parts["mq"], parts["mk"], parts["mv"], parts["mo"], parts["kc"], parts["vc"],
            _pad_cols(gt[:, 0], LANE), _pad_cols(gt[:, 1], LANE),
            _pad_cols(jnp.concatenate([parts["mi"], parts["mf"]], axis=1), LANE)]
    return jnp.concatenate(cols, axis=1).astype(bf16)


def _block_diag2(w):
    z = jnp.zeros_like(w)
    return jnp.concatenate([jnp.concatenate([w, z], axis=-1), jnp.concatenate([z, w], axis=-1)], axis=-2)


def kernel(x, mem, g_mix, w_in, cmp_pos_k, cmp_w1_k, cmp_w2_k, cmp_pos_v, cmp_w1_v, cmp_w2_v, mlstm_conv_w, mlstm_b_i, mlstm_b_f, g_head_nsa, g_head_mlstm, w_out, g_mem_q, g_mem_kv, w_mem_q, w_mem_k, w_mem_v, w_mem_o, g_ffn, w_up, ffn_conv_w, w_down, g_final):
    b, s, _ = x.shape
    assert b == 1 and g_mix.shape[0] == 1
    sel_tk = 2048
    assert s % 8192 == 0 and s >= sel_tk + WINDOW
    l = 0
    xs = x[0]
    row = lambda a: a.reshape(1, -1)

    qa, kst, vs, kwt, vw, pb = _inproj(xs, row(g_mix[l]), _build_w_in(w_in[l]), sel_tk // SEL_BLOCK)

    n16 = s // CMP_STRIDE
    dup = lambda p: jnp.concatenate([p, p], axis=-1)
    pos = jnp.stack([dup(cmp_pos_k[l]), dup(cmp_pos_v[l])])
    w1 = _block_diag2(jnp.stack([cmp_w1_k[l], cmp_w1_v[l]]).reshape(2, CMP_LEN, NSA_DH, -1)).astype(bf16)
    w2 = _block_diag2(jnp.stack([cmp_w2_k[l], cmp_w2_v[l]])).astype(bf16)
    kct, vc = _compress(pb, pos, w1, w2)

    n_sel = s // SEL_BLOCK
    cstart = jnp.arange(n16)[:, None] * CMP_STRIDE
    sstart = jnp.arange(n_sel)[None, :] * SEL_BLOCK
    overlap = ((cstart < sstart + SEL_BLOCK) & (cstart + CMP_LEN > sstart)).astype(bf16)
    oc, sel = _cmp_attention(qa, kct, vc, overlap)

    gh_pad = _pad_heads(row(g_head_nsa[l]), NSA_HEADS, NSA_DH)
    nsa_n = _selwin_attention(qa, pb, oc, sel, gh_pad, kst, vs, kwt, vw, tk=sel_tk)

    bias_if = _pad_cols(jnp.concatenate([row(mlstm_b_i[l]), row(mlstm_b_f[l])], axis=1), LANE)
    ml = _mlstm(pb, mlstm_conv_w[l], bias_if, row(g_head_mlstm[l]))

    w_out_nsa = jnp.pad(w_out[l][:NSA_HEADS * NSA_DH].reshape(NSA_HEADS, NSA_DH, D_MODEL),
                        ((0, 0), (0, LANE - NSA_DH), (0, 0))).reshape(NSA_HEADS * LANE, D_MODEL).astype(bf16)
    w_out_m = w_out[l][NSA_HEADS * NSA_DH:].astype(bf16)
    k_mem, v_mem = _memkv(mem[0], row(g_mem_kv[l]), w_mem_k[l].astype(bf16), w_mem_v[l].astype(bf16))
    x2 = _mix_mem(xs, nsa_n, ml, w_out_nsa, w_out_m, row(g_mem_q[l]), w_mem_q[l].astype(bf16),
                  k_mem.T, v_mem, w_mem_o[l].astype(bf16))

    out = _ffn(x2, row(g_ffn[l]), w_up[l].astype(bf16), ffn_conv_w[l], w_down[l].astype(bf16), row(g_final))
    return out[None]
```

```python
import functools

import jax
import jax.numpy as jnp
from jax import lax
from jax.experimental import pallas as pl
from jax.experimental.pallas import tpu as pltpu

f32 = jnp.float32
bf16 = jnp.bfloat16

D_MODEL = 1024
NSA_HEADS = 8
NSA_GROUPS = 2
NSA_HPG = NSA_HEADS // NSA_GROUPS
NSA_DH = 64
CMP_LEN = 32
CMP_STRIDE = 16
SEL_BLOCK = 64
SEL_TOPK = 16
WINDOW = 512
FORCE_BONUS = 1.0e4
M_HEADS = 4
M_DH = 128
M_WIDTH = M_HEADS * M_DH
M_CONV = 4
M_CHUNK = 64
MEM_HEADS = 4
MEM_DH = D_MODEL // MEM_HEADS
D_FF = 2816
F_CONV = 3
EPS = 1e-6
NSA_QSCALE = NSA_DH ** -0.5 * 1.4426950408889634

LANE = 128
SUBLANE = 8
NEG = -1e30
VMEM_LIMIT = 56 * 1024 * 1024

A_Q, A_KS, A_VS, A_KW, A_VW, NA = 0, 1024, 1152, 1280, 1408, 1536
B_MQ, B_MV, B_MO, B_KC, B_VC, B_G0, B_IF, NB = 0, 1024, 1536, 2048, 2176, 2304, 2560, 2688


def _dot(a, b):
    return jnp.dot(a, b, preferred_element_type=f32)


def _rms(x, g):
    return x * lax.rsqrt(jnp.mean(x * x, axis=-1, keepdims=True) + EPS) * g


def _gelu(x):
    return 0.5 * x * (1.0 + lax.erf(x * 0.7071067811865476))


def _params(*sem):
    return pltpu.CompilerParams(dimension_semantics=sem, vmem_limit_bytes=VMEM_LIMIT)


def _const_spec(shape):
    nd = len(shape)
    return pl.BlockSpec(shape, lambda *_: (0,) * nd, pipeline_mode=pl.Buffered(1))


def _split_keys_t(ref, k, extra=None):
    kt = k.T
    n_extra = 0 if extra is None else extra.shape[0]
    for g in range(NSA_GROUPS):
        ref[g, 0:NSA_DH, :] = kt[g * NSA_DH:(g + 1) * NSA_DH].astype(bf16)
        if extra is not None:
            ref[g, NSA_DH:NSA_DH + n_extra, :] = extra
        ref[g, NSA_DH + n_extra:, :] = jnp.zeros((LANE - NSA_DH - n_extra, kt.shape[1]), bf16)


def _split_values(ref, v, ones_col):
    lane = lax.broadcasted_iota(jnp.int32, v.shape, 1)
    fill = jnp.where(lane == NSA_DH, 1.0, 0.0) if ones_col else 0.0
    for g in range(NSA_GROUPS):
        vg = v if g == 0 else pltpu.roll(v, LANE - g * NSA_DH, 1)
        ref[g] = jnp.where(lane < NSA_DH, vg, fill).astype(bf16)


def _inproj_kernel(x_ref, g_ref, w_ref, q_ref, kst_ref, vs_ref, kwt_ref, vw_ref, pb_ref, *, tm, nblk):
    h = _rms(x_ref[...], g_ref[...]).astype(bf16)
    q_ref[...] = (_dot(h, w_ref[:, A_Q:A_KS]) * NSA_QSCALE).astype(bf16)
    kv = _dot(h, w_ref[:, A_KS:NA])
    key = pl.program_id(0) * tm + lax.broadcasted_iota(jnp.int32, (nblk, tm), 1)
    rowj = lax.broadcasted_iota(jnp.int32, (nblk, tm), 0)
    ind = ((lax.shift_right_logical(key, 6) & (nblk - 1)) == rowj).astype(bf16)
    _split_keys_t(kst_ref, kv[:, 0:LANE], ind)
    _split_values(vs_ref, kv[:, LANE:2 * LANE], True)
    _split_keys_t(kwt_ref, kv[:, 2 * LANE:3 * LANE])
    _split_values(vw_ref, kv[:, 3 * LANE:4 * LANE], True)
    for c0 in range(0, NB, 512):
        c1 = min(c0 + 512, NB)
        pb_ref[:, c0:c1] = _dot(h, w_ref[:, NA + c0:NA + c1])


def _inproj(x, g, w, nblk, tm=512):
    s = x.shape[0]
    assert nblk & (nblk - 1) == 0 and SEL_BLOCK == 64
    kspec = pl.BlockSpec((NSA_GROUPS, LANE, tm), lambda i: (0, 0, i))
    vspec = pl.BlockSpec((NSA_GROUPS, tm, LANE), lambda i: (0, i, 0))
    kshape = jax.ShapeDtypeStruct((NSA_GROUPS, LANE, s), bf16)
    vshape = jax.ShapeDtypeStruct((NSA_GROUPS, s, LANE), bf16)
    return pl.pallas_call(
        functools.partial(_inproj_kernel, tm=tm, nblk=nblk),
        grid=(s // tm,),
        in_specs=[pl.BlockSpec((tm, D_MODEL), lambda i: (i, 0)),
                  _const_spec((1, D_MODEL)),
                  _const_spec((D_MODEL, NA + NB))],
        out_specs=[pl.BlockSpec((tm, NSA_HEADS * LANE), lambda i: (i, 0)), kspec, vspec, kspec, vspec,
                   pl.BlockSpec((tm, NB), lambda i: (i, 0))],
        out_shape=[jax.ShapeDtypeStruct((s, NSA_HEADS * LANE), bf16), kshape, vshape, kshape, vshape,
                   jax.ShapeDtypeStruct((s, NB), f32)],
        compiler_params=_params("parallel"),
        name="inproj",
    )(x, g, w)


def _compress_kernel(kc_ref, vc_ref, pos_ref, w1_ref, w2_ref, kct_ref, vcp_ref):
    n = kc_ref.shape[0] // CMP_STRIDE

    def mlp(x_ref, a):
        top = bot = None
        for l in range(CMP_STRIDE):
            xl = x_ref[pl.ds(l, n, stride=CMP_STRIDE), :]
            t = _dot((xl + pos_ref[a, l:l + 1, :]).astype(bf16), w1_ref[a, l])
            b = _dot((xl + pos_ref[a, CMP_STRIDE + l:CMP_STRIDE + l + 1, :]).astype(bf16), w1_ref[a, CMP_STRIDE + l])
            top = t if top is None else top + t
            bot = b if bot is None else bot + b
        pre = top + pltpu.roll(bot, n - 1, 0)
        return _dot(_gelu(pre).astype(bf16), w2_ref[a])

    _split_keys_t(kct_ref, mlp(kc_ref, 0))
    _split_values(vcp_ref, mlp(vc_ref, 1), False)


def _compress(pb, pos, w1, w2):
    s = pb.shape[0]
    n = s // CMP_STRIDE
    return pl.pallas_call(
        _compress_kernel,
        grid=(1,),
        in_specs=[pl.BlockSpec((s, LANE), lambda i: (0, B_KC // LANE)),
                  pl.BlockSpec((s, LANE), lambda i: (0, B_VC // LANE)),
                  _const_spec(pos.shape), _const_spec(w1.shape), _const_spec(w2.shape)],
        out_specs=[pl.BlockSpec((NSA_GROUPS, LANE, n), lambda i: (0, 0, 0)),
                   pl.BlockSpec((NSA_GROUPS, n, LANE), lambda i: (0, 0, 0))],
        out_shape=[jax.ShapeDtypeStruct((NSA_GROUPS, LANE, n), bf16),
                   jax.ShapeDtypeStruct((NSA_GROUPS, n, LANE), bf16)],
        compiler_params=_params("arbitrary"),
        name="compress",
    )(pb, pb, pos, w1, w2)


def _cmp_kernel(q_ref, kct_ref, vc_ref, ov_ref, oc_ref, sel_ref, *, tq):
    ncmp = kct_ref.shape[2]
    nsel = ov_ref.shape[1]
    t0 = pl.program_id(0) * tq
    tpos = t0 + lax.broadcasted_iota(jnp.int32, (tq, 1), 0)
    cend = lax.broadcasted_iota(jnp.int32, (1, ncmp), 1) * CMP_STRIDE + (CMP_LEN - 1)
    vis = cend <= tpos
    any_vis = tpos >= CMP_LEN - 1
    cur = lax.shift_right_logical(tpos, 6)
    blk = lax.broadcasted_iota(jnp.int32, (1, nsel), 1)
    forced = (blk == 0) | (blk == cur) | (blk == cur - 1)
    scores = []
    qk = lambda h: _dot(q_ref[:, h * LANE:(h + 1) * LANE], kct_ref[h // NSA_HPG])
    ahead = 2
    pending = [qk(h) for h in range(ahead)]
    for g in range(NSA_GROUPS):
        psum = None
        for hh in range(NSA_HPG):
            h = g * NSA_HPG + hh
            c0 = h * LANE
            s = jnp.where(vis, pending.pop(0), NEG)
            if h + ahead < NSA_HEADS:
                pending.append(qk(h + ahead))
            e = jnp.exp2(s - jnp.max(s, axis=-1, keepdims=True))
            l = jnp.sum(e, axis=-1, keepdims=True)
            p = e * jnp.where(any_vis, 1.0 / l, 0.0)
            oc_ref[:, c0:c0 + LANE] = _dot(p.astype(bf16), vc_ref[g])
            psum = p if psum is None else psum + p
        imp = _dot(psum.astype(bf16), ov_ref[...])
        scores.append(jnp.where(forced, -jnp.inf, jnp.where(blk <= cur, imp, -FORCE_BONUS)))
    score = jnp.concatenate(scores, axis=0).T
    blkt = lax.broadcasted_iota(jnp.int32, score.shape, 0).astype(f32)
    for _ in range(SEL_TOPK - 3):
        mx = jnp.max(score, axis=0, keepdims=True)
        j = jnp.min(jnp.where(score == mx, blkt, float(nsel)), axis=0, keepdims=True)
        score = jnp.where(blkt == j, -jnp.inf, score)
    selm = (score == -jnp.inf).astype(f32).T
    for g in range(NSA_GROUPS):
        sel_ref[g] = selm[g * tq:(g + 1) * tq]


def _cmp_attention(pa, kct, vc, ov, tq=512):
    s = pa.shape[0]
    ncmp, nsel = ov.shape
    return pl.pallas_call(
        functools.partial(_cmp_kernel, tq=tq),
        grid=(s // tq,),
        in_specs=[pl.BlockSpec((tq, NSA_HEADS * LANE), lambda i: (i, 0)),
                  _const_spec((NSA_GROUPS, LANE, ncmp)),
                  _const_spec((NSA_GROUPS, ncmp, LANE)),
                  _const_spec((ncmp, nsel))],
        out_specs=[pl.BlockSpec((tq, NSA_HEADS * LANE), lambda i: (i, 0)),
                   pl.BlockSpec((NSA_GROUPS, tq, nsel), lambda i: (0, i, 0))],
        out_shape=[jax.ShapeDtypeStruct((s, NSA_HEADS * LANE), f32),
                   jax.ShapeDtypeStruct((NSA_GROUPS, s, nsel), f32)],
        compiler_params=_params("parallel"),
        name="cmp_attn",
    )(pa, kct, vc, ov)


def _selwin_kernel(q_ref, oc_ref, sel_ref, gate_ref, gh_ref, kst_ref, vs_ref, kwt_ref, vw_ref, out_ref, *, tq, tk):
    nsel = sel_ref.shape[2]
    nblk = tk // SEL_BLOCK
    t0 = pl.program_id(1) * tq
    tpos = t0 + lax.broadcasted_iota(jnp.int32, (tq, 1), 0)
    q4 = [q_ref[:, hh * LANE:(hh + 1) * LANE] for hh in range(NSA_HPG)]
    sel = sel_ref[0]
    lane = lax.broadcasted_iota(jnp.int32, (tq, LANE), 1)
    bias_lanes = (lane >= NSA_DH) & (lane < NSA_DH + nblk)

    def q_biased(kt):
        sh = lax.rem(NSA_DH + nsel - kt * nblk, jnp.int32(nsel))
        bias = ((pltpu.roll(sel, sh, 1)[:, :LANE] - 1.0) * -NEG).astype(bf16)
        return [jnp.where(bias_lanes, bias, qh) for qh in q4]

    def attend(qb, st, width, carry, causal):
        kblk = kst_ref[0, :, pl.ds(st, width)]
        vblk = vs_ref[0, pl.ds(st, width), :]
        if causal:
            keep = st + lax.broadcasted_iota(jnp.int32, (1, width), 1) <= tpos
        out = []
        s_next = _dot(qb[0], kblk)
        for hh in range(NSA_HPG):
            m, acc = carry[hh]
            s = s_next
            if hh + 1 < NSA_HPG:
                s_next = _dot(qb[hh + 1], kblk)
            if causal:
                s = jnp.where(keep, s, NEG)
            mn = jnp.maximum(m, jnp.max(s, axis=-1, keepdims=True))
            p = jnp.exp2(s - mn)
            out.append((mn, jnp.exp2(m - mn) * acc + _dot(p.astype(bf16), vblk)))
        return tuple(out)

    last = t0 // tk
    init = tuple((jnp.full((tq, 1), NEG, f32), jnp.zeros((tq, LANE), f32)) for _ in range(NSA_HPG))
    carry = lax.fori_loop(
        0, last, lambda kt, c: attend(q_biased(kt), pl.multiple_of(kt * tk, tk), tk, c, False), init)
    qb_last = q_biased(last)
    base = last * tk
    carry = lax.fori_loop(
        0, (t0 - base) // tq,
        lambda j, c: attend(qb_last, pl.multiple_of(base + j * tq, tq), tq, c, False), carry)

    wlen = tq + WINDOW
    ws = pl.multiple_of(jnp.maximum(t0 - WINDOW, 0), LANE)
    kdiag = kst_ref[0, :, pl.ds(pl.multiple_of(t0, tq), tq)]
    vdiag = vs_ref[0, pl.ds(pl.multiple_of(t0, tq), tq), :]
    kwin = kwt_ref[0, :, pl.ds(ws, wlen)]
    vwin = vw_ref[0, pl.ds(ws, wlen), :]
    dmask = t0 + lax.broadcasted_iota(jnp.int32, (1, tq), 1) <= tpos
    kpos = ws + lax.broadcasted_iota(jnp.int32, (1, wlen), 1)
    wmask = (kpos <= tpos) & (kpos > tpos - WINDOW)
    chains = [(br, hh) for hh in range(NSA_HPG) for br in ("diag", "win")]
    qk = lambda br, hh: _dot(qb_last[hh], kdiag) if br == "diag" else _dot(q4[hh], kwin)

    gates = jax.nn.sigmoid(gate_ref[...])
    valid = lane < NSA_DH
    ahead = 2
    pending = [qk(*c) for c in chains[:ahead]]
    acc_s = None
    for n, (br, hh) in enumerate(chains):
        s = pending.pop(0)
        if n + ahead < len(chains):
            pending.append(qk(*chains[n + ahead]))
        if br == "diag":
            m, acc = carry[hh]
            s = jnp.where(dmask, s, NEG)
            mn = jnp.maximum(m, jnp.max(s, axis=-1, keepdims=True))
            acc_s = jnp.exp2(m - mn) * acc + _dot(jnp.exp2(s - mn).astype(bf16), vdiag)
            continue
        s = jnp.where(wmask, s, NEG)
        pw = jnp.exp2(s - jnp.max(s, axis=-1, keepdims=True))
        acc_w = _dot(pw.astype(bf16), vwin)
        o_s = acc_s / acc_s[:, NSA_DH:NSA_DH + 1]
        o_w = acc_w / acc_w[:, NSA_DH:NSA_DH + 1]
        o_c = oc_ref[:, hh * LANE:(hh + 1) * LANE]
        o = (gates[:, 3 * hh:3 * hh + 1] * o_c + gates[:, 3 * hh + 1:3 * hh + 2] * o_s
             + gates[:, 3 * hh + 2:3 * hh + 3] * o_w)
        o = jnp.where(valid, o, 0.0)
        ms = jnp.sum(o * o, axis=-1, keepdims=True) * (1.0 / NSA_DH)
        y = o * lax.rsqrt(ms + EPS) * gh_ref[:, hh * LANE:(hh + 1) * LANE]
        out_ref[:, hh * LANE:(hh + 1) * LANE] = y.astype(bf16)


def _selwin_attention(pa, pb, oc, sel, gh_pad, kst, vs, kwt, vw, tq=512, tk=2048):
    s = pa.shape[0]
    nsel = sel.shape[2]
    gw = NSA_HPG * LANE
    assert tk % tq == 0 and s % tk == 0
    kspec = pl.BlockSpec((1, LANE, s), lambda g, i: (g, 0, 0), pipeline_mode=pl.Buffered(1))
    vspec = pl.BlockSpec((1, s, LANE), lambda g, i: (g, 0, 0), pipeline_mode=pl.Buffered(1))
    return pl.pallas_call(
        functools.partial(_selwin_kernel, tq=tq, tk=tk),
        grid=(NSA_GROUPS, s // tq),
        in_specs=[pl.BlockSpec((tq, gw), lambda g, i: (i, g)),
                  pl.BlockSpec((tq, gw), lambda g, i: (i, g)),
                  pl.BlockSpec((1, tq, nsel), lambda g, i: (g, i, 0)),
                  pl.BlockSpec((tq, LANE), lambda g, i: (i, B_G0 // LANE + g)),
                  pl.BlockSpec((1, gw), lambda g, i: (0, g)),
                  kspec, vspec, kspec, vspec],
        out_specs=pl.BlockSpec((tq, gw), lambda g, i: (i, g)),
        out_shape=jax.ShapeDtypeStruct((s, NSA_HEADS * LANE), bf16),
        compiler_params=_params("parallel", "parallel"),
        name="selwin_attn",
    )(pa, oc, sel, pb, gh_pad, kst, vs, kwt, vw)


def _mlstm_kernel(qk_ref, v_ref, og_ref, if_ref, cw_ref, bias_ref, gh_ref, out_ref, buf, cst, mst, *, tm):
    @pl.when(pl.program_id(0) == 0)
    def _():
        buf[0:SUBLANE, :] = jnp.zeros((SUBLANE, 2 * M_WIDTH), f32)
        cst[...] = jnp.zeros(cst.shape, f32)
        mst[...] = jnp.zeros(mst.shape, f32)

    qk = qk_ref[...]
    buf[SUBLANE:SUBLANE + tm, :] = qk
    conv = cw_ref[M_CONV - 1:M_CONV, :] * qk
    for k in range(M_CONV - 1):
        off = SUBLANE - (M_CONV - 1) + k
        conv = conv + cw_ref[k:k + 1, :] * buf[off:off + tm, :]
    buf[0:SUBLANE, :] = qk[tm - SUBLANE:tm]
    qkc = conv * jax.nn.sigmoid(conv)
    qs = (qkc[:, :M_WIDTH] * (M_DH ** -0.5)).astype(bf16)
    ks = qkc[:, M_WIDTH:]
    vb = v_ref[...].astype(bf16)
    ogate = jax.nn.sigmoid(og_ref[...])

    pre = if_ref[...] + bias_ref[...]
    lane = lax.broadcasted_iota(jnp.int32, (M_CHUNK, LANE), 1)
    rowi = lax.broadcasted_iota(jnp.int32, (M_CHUNK, LANE), 0)
    logf = jnp.minimum(pre, 0.0) - jnp.log1p(jnp.exp(-jnp.abs(pre)))
    tri = (lax.broadcasted_iota(jnp.int32, (M_CHUNK, M_CHUNK), 0)
           >= lax.broadcasted_iota(jnp.int32, (M_CHUNK, M_CHUNK), 1))
    ones_col = (lax.broadcasted_iota(jnp.int32, (M_CHUNK, LANE), 1) == 0).astype(bf16)

    for c in range(tm // M_CHUNK):
        r0, r1 = c * M_CHUNK, (c + 1) * M_CHUNK
        cum = logf[r0:r1]
        sft = 1
        while sft < M_CHUNK:
            cum = cum + jnp.where(rowi >= sft, pltpu.roll(cum, sft, 0), 0.0)
            sft *= 2
        comb = jnp.where(lane < M_HEADS, pre[r0:r1], cum)
        comb_t = comb.T
        for h in range(M_HEADS):
            hs = slice(h * M_DH, (h + 1) * M_DH)
            bcol = comb[:, M_HEADS + h:M_HEADS + h + 1]
            icol = comb[:, h:h + 1]
            brow = comb_t[M_HEADS + h:M_HEADS + h + 1, :]
            irow = comb_t[h:h + 1, :]
            mprev = mst[0:1, h:h + 1]
            dmat = jnp.where(tri, bcol - brow + irow, -jnp.inf)
            m_inter = bcol + mprev
            m_t = jnp.maximum(jnp.max(dmat, axis=-1, keepdims=True), m_inter)
            a_inter = jnp.exp(m_inter - m_t)
            qh = qs[r0:r1, hs]
            kh = ks[r0:r1, hs]
            v_aug = jnp.concatenate([vb[r0:r1, hs], ones_col], axis=1)
            qk_s = lax.dot_general(qh, kh.astype(bf16), (((1,), (1,)), ((), ())), preferred_element_type=f32)
            sc = qk_s * jnp.exp(dmat - m_t)
            tot = a_inter * _dot(qh, cst[h].astype(bf16)) + _dot(sc.astype(bf16), v_aug)
            den = tot[:, M_DH:M_DH + 1]
            hout = tot[:, :M_DH] / jnp.maximum(jnp.abs(den), jnp.exp(-m_t))
            b_last = bcol[M_CHUNK - 1:M_CHUNK, :]
            g_s = b_last - bcol + icol
            m_new = jnp.maximum(b_last + mprev, jnp.max(g_s, axis=0, keepdims=True))
            w_s = jnp.exp(g_s - m_new)
            decay = jnp.exp(b_last + mprev - m_new)
            upd = lax.dot_general((kh * w_s).astype(bf16), v_aug, (((0,), (0,)), ((), ())),
                                  preferred_element_type=f32)
            cst[h] = decay * cst[h] + upd
            mst[0:1, h:h + 1] = m_new
            hn = hout * lax.rsqrt(jnp.mean(hout * hout, axis=-1, keepdims=True) + EPS) * gh_ref[:, hs]
            out_ref[r0:r1, hs] = (ogate[r0:r1, hs] * hn).astype(bf16)


def _mlstm(pb, conv_w, bias_if, g_head, tm=256):
    s = pb.shape[0]
    return pl.pallas_call(
        functools.partial(_mlstm_kernel, tm=tm),
        grid=(s // tm,),
        in_specs=[pl.BlockSpec((tm, 2 * M_WIDTH), lambda i: (i, B_MQ // (2 * M_WIDTH))),
                  pl.BlockSpec((tm, M_WIDTH), lambda i: (i, B_MV // M_WIDTH)),
                  pl.BlockSpec((tm, M_WIDTH), lambda i: (i, B_MO // M_WIDTH)),
                  pl.BlockSpec((tm, LANE), lambda i: (i, B_IF // LANE)),
                  _const_spec((M_CONV, 2 * M_WIDTH)),
                  _const_spec((1, LANE)),
                  _const_spec((1, M_WIDTH))],
        out_specs=pl.BlockSpec((tm, M_WIDTH), lambda i: (i, 0)),
        out_shape=jax.ShapeDtypeStruct((s, M_WIDTH), bf16),
        scratch_shapes=[pltpu.VMEM((tm + SUBLANE, 2 * M_WIDTH), f32),
                        pltpu.VMEM((M_HEADS, M_DH, 2 * M_DH), f32),
                        pltpu.VMEM((SUBLANE, LANE), f32)],
        compiler_params=_params("arbitrary"),
        name="mlstm",
    )(pb, pb, pb, pb, conv_w, bias_if, g_head)


def _memkv_kernel(mem_ref, g_ref, wk_ref, wv_ref, k_ref, v_ref):
    hm = _rms(mem_ref[...], g_ref[...]).astype(bf16)
    k_ref[...] = _dot(hm, wk_ref[...]).astype(bf16)
    v_ref[...] = _dot(hm, wv_ref[...]).astype(bf16)


def _memkv(mem, g, wk, wv):
    m = mem.shape[0]
    return pl.pallas_call(
        _memkv_kernel,
        out_shape=[jax.ShapeDtypeStruct((m, D_MODEL), bf16)] * 2,
        compiler_params=pltpu.CompilerParams(vmem_limit_bytes=VMEM_LIMIT),
        name="mem_kv",
    )(mem, g, wk, wv)


def _mix_mem_kernel(x_ref, nsa_ref, ml_ref, won_ref, wom_ref, gq_ref, wq_ref, kt_ref, v_ref, wo_ref, o_ref):
    x1 = x_ref[...] + _dot(nsa_ref[...], won_ref[...]) + _dot(ml_ref[...], wom_ref[...])
    hq = _rms(x1, gq_ref[...]).astype(bf16)
    q = (_dot(hq, wq_ref[...]) * (MEM_DH ** -0.5)).astype(bf16)
    outs = []
    for h in range(MEM_HEADS):
        hs = slice(h * MEM_DH, (h + 1) * MEM_DH)
        s = _dot(q[:, hs], kt_ref[hs, :])
        e = jnp.exp(s - jnp.max(s, axis=-1, keepdims=True))
        p = e / jnp.sum(e, axis=-1, keepdims=True)
        outs.append(_dot(p.astype(bf16), v_ref[:, hs]).astype(bf16))
    o_ref[...] = x1 + _dot(jnp.concatenate(outs, axis=1), wo_ref[...])


def _mix_mem(x, nsa_n, ml, w_out_nsa, w_out_m, g_q, w_q, k_t, v, w_o, tm=512):
    s = x.shape[0]
    m = v.shape[0]
    return pl.pallas_call(
        _mix_mem_kernel,
        grid=(s // tm,),
        in_specs=[pl.BlockSpec((tm, D_MODEL), lambda i: (i, 0)),
                  pl.BlockSpec((tm, NSA_HEADS * LANE), lambda i: (i, 0)),
                  pl.BlockSpec((tm, M_WIDTH), lambda i: (i, 0)),
                  _const_spec((NSA_HEADS * LANE, D_MODEL)),
                  _const_spec((M_WIDTH, D_MODEL)),
                  _const_spec((1, D_MODEL)),
                  _const_spec((D_MODEL, D_MODEL)),
                  _const_spec((D_MODEL, m)),
                  _const_spec((m, D_MODEL)),
                  _const_spec((D_MODEL, D_MODEL))],
        out_specs=pl.BlockSpec((tm, D_MODEL), lambda i: (i, 0)),
        out_shape=jax.ShapeDtypeStruct((s, D_MODEL), f32),
        compiler_params=_params("parallel"),
        name="mix_mem",
    )(x, nsa_n, ml, w_out_nsa, w_out_m, g_q, w_q, k_t, v, w_o)


def _ffn_kernel(x_ref, xh_ref, g_ref, wup_ref, cw_ref, wdn_ref, gf_ref, o_ref, *, tm, widths):
    x = x_ref[...]
    hf = _rms(x, g_ref[...]).astype(bf16)
    hist = jnp.where(pl.program_id(0) > 0, 1.0, 0.0)
    hh = (_rms(xh_ref[...], g_ref[...]) * hist).astype(bf16)

    def conv_cols(c0, cw):
        w = wup_ref[:, c0:c0 + cw]
        u = _dot(hf, w)
        uh = _dot(hh, w)
        row = lax.broadcasted_iota(jnp.int32, (tm, cw), 0)
        y = cw_ref[F_CONV - 1:F_CONV, c0:c0 + cw] * u
        for k in range(F_CONV - 1):
            d = F_CONV - 1 - k
            sh = pltpu.roll(u, d, 0)
            for r in range(d):
                sh = jnp.where(row == r, uh[SUBLANE - d + r:SUBLANE - d + r + 1, :], sh)
            y = y + cw_ref[k:k + 1, c0:c0 + cw] * sh
        return y

    starts = [sum(widths[:n]) for n in range(len(widths))]
    both = lambda n: (conv_cols(starts[n], widths[n]), conv_cols(D_FF + starts[n], widths[n]))
    acc = x
    nxt = both(0)
    for n, (c0, cw) in enumerate(zip(starts, widths)):
        ya, yb = nxt
        if n + 1 < len(widths):
            nxt = both(n + 1)
        gated = _gelu(ya) * yb
        acc = acc + _dot(gated.astype(bf16), wdn_ref[c0:c0 + cw, :])
    o_ref[...] = _rms(acc, gf_ref[...])


def _ffn(x, g, w_up, conv_w, w_down, g_final, tm=1024, widths=(1280, 1536)):
    s = x.shape[0]
    halo = tm // SUBLANE
    assert sum(widths) == D_FF
    return pl.pallas_call(
        functools.partial(_ffn_kernel, tm=tm, widths=widths),
        grid=(s // tm,),
        in_specs=[pl.BlockSpec((tm, D_MODEL), lambda i: (i, 0)),
                  pl.BlockSpec((SUBLANE, D_MODEL), lambda i: (jnp.maximum(i * halo - 1, 0), 0)),
                  _const_spec((1, D_MODEL)),
                  _const_spec((D_MODEL, 2 * D_FF)),
                  _const_spec((F_CONV, 2 * D_FF)),
                  _const_spec((D_FF, D_MODEL)),
                  _const_spec((1, D_MODEL))],
        out_specs=pl.BlockSpec((tm, D_MODEL), lambda i: (i, 0)),
        out_shape=jax.ShapeDtypeStruct((s, D_MODEL), f32),
        compiler_params=_params("parallel"),
        name="ffn",
    )(x, x, g, w_up, conv_w, w_down, g_final)


def _pad_cols(a, n):
    return jnp.pad(a, ((0, 0), (0, n - a.shape[1])))


def _pad_heads(a, heads, dh):
    r = a.shape[0]
    return jnp.pad(a.reshape(r, heads, dh), ((0, 0), (0, 0), (0, LANE - dh))).reshape(r, heads * LANE)


def _build_w_in(w):
    o = 0
    parts = {}
    sizes = (("q", 512), ("kc", 128), ("vc", 128), ("ks", 128), ("vs", 128), ("kw", 128), ("vw", 128),
             ("gt", 24), ("mq", 512), ("mk", 512), ("mv", 512), ("mo", 512), ("mi", 4), ("mf", 4))
    for name, n in sizes:
        parts[name] = w[:, o:o + n]
        o += n
    gt = parts["gt"].reshape(D_MODEL, NSA_GROUPS, NSA_HPG * 3)
    cols = [_pad_heads(parts["q"], NSA_HEADS, NSA_DH), parts["ks"], parts["vs"], parts["kw"], parts["vw"],
            parts["mq"], parts["mk"], parts["mv"], parts["mo"], parts["kc"], parts["vc"],
            _pad_cols(gt[:, 0], LANE), _pad_cols(gt[:, 1], LANE),
            _pad_cols(jnp.concatenate([parts["mi"], parts["mf"]], axis=1), LANE)]
    return jnp.concatenate(cols, axis=1).astype(bf16)


def _block_diag2(w):
    z = jnp.zeros_like(w)
    return jnp.concatenate([jnp.concatenate([w, z], axis=-1), jnp.concatenate([z, w], axis=-1)], axis=-2)


def kernel(x, mem, g_mix, w_in, cmp_pos_k, cmp_w1_k, cmp_w2_k, cmp_pos_v, cmp_w1_v, cmp_w2_v, mlstm_conv_w, mlstm_b_i, mlstm_b_f, g_head_nsa, g_head_mlstm, w_out, g_mem_q, g_mem_kv, w_mem_q, w_mem_k, w_mem_v, w_mem_o, g_ffn, w_up, ffn_conv_w, w_down, g_final):
    b, s, _ = x.shape
    assert b == 1 and g_mix.shape[0] == 1
    sel_tk = 2048
    assert s % 8192 == 0 and s >= sel_tk + WINDOW
    l = 0
    xs = x[0]
    row = lambda a: a.reshape(1, -1)

    qa, kst, vs, kwt, vw, pb = _inproj(xs, row(g_mix[l]), _build_w_in(w_in[l]), sel_tk // SEL_BLOCK)

    n16 = s // CMP_STRIDE
    dup = lambda p: jnp.concatenate([p, p], axis=-1)
    pos = jnp.stack([dup(cmp_pos_k[l]), dup(cmp_pos_v[l])])
    w1 = _block_diag2(jnp.stack([cmp_w1_k[l], cmp_w1_v[l]]).reshape(2, CMP_LEN, NSA_DH, -1)).astype(bf16)
    w2 = _block_diag2(jnp.stack([cmp_w2_k[l], cmp_w2_v[l]])).astype(bf16)
    kct, vc = _compress(pb, pos, w1, w2)

    n_sel = s // SEL_BLOCK
    cstart = jnp.arange(n16)[:, None] * CMP_STRIDE
    sstart = jnp.arange(n_sel)[None, :] * SEL_BLOCK
    overlap = ((cstart < sstart + SEL_BLOCK) & (cstart + CMP_LEN > sstart)).astype(bf16)
    oc, sel = _cmp_attention(qa, kct, vc, overlap)

    gh_pad = _pad_heads(row(g_head_nsa[l]), NSA_HEADS, NSA_DH)
    nsa_n = _selwin_attention(qa, pb, oc, sel, gh_pad, kst, vs, kwt, vw, tk=sel_tk)

    bias_if = _pad_cols(jnp.concatenate([row(mlstm_b_i[l]), row(mlstm_b_f[l])], axis=1), LANE)
    ml = _mlstm(pb, mlstm_conv_w[l], bias_if, row(g_head_mlstm[l]))

    w_out_nsa = jnp.pad(w_out[l][:NSA_HEADS * NSA_DH].reshape(NSA_HEADS, NSA_DH, D_MODEL),
                        ((0, 0), (0, LANE - NSA_DH), (0, 0))).reshape(NSA_HEADS * LANE, D_MODEL).astype(bf16)
    w_out_m = w_out[l][NSA_HEADS * NSA_DH:].astype(bf16)
    k_mem, v_mem = _memkv(mem[0], row(g_mem_kv[l]), w_mem_k[l].astype(bf16), w_mem_v[l].astype(bf16))
    x2 = _mix_mem(xs, nsa_n, ml, w_out_nsa, w_out_m, row(g_mem_q[l]), w_mem_q[l].astype(bf16),
                  k_mem.T, v_mem, w_mem_o[l].astype(bf16))

    out = _ffn(x2, row(g_ffn[l]), w_up[l].astype(bf16), ffn_conv_w[l], w_down[l].astype(bf16), row(g_final))
    return out[None]
```

```python
import functools

import jax
import jax.numpy as jnp
from jax import lax
from jax.experimental import pallas as pl
from jax.experimental.pallas import tpu as pltpu

f32 = jnp.float32
bf16 = jnp.bfloat16

D_MODEL = 1024
NSA_HEADS = 8
NSA_GROUPS = 2
NSA_HPG = NSA_HEADS // NSA_GROUPS
NSA_DH = 64
CMP_LEN = 32
CMP_STRIDE = 16
SEL_BLOCK = 64
SEL_TOPK = 16
WINDOW = 512
FORCE_BONUS = 1.0e4
M_HEADS = 4
M_DH = 128
M_WIDTH = M_HEADS * M_DH
M_CONV = 4
M_CHUNK = 64
MEM_HEADS = 4
MEM_DH = D_MODEL // MEM_HEADS
D_FF = 2816
F_CONV = 3
EPS = 1e-6
NSA_QSCALE = NSA_DH ** -0.5 * 1.4426950408889634

LANE = 128
SUBLANE = 8
NEG = -1e30
VMEM_LIMIT = 56 * 1024 * 1024

A_Q, A_KS, A_VS, A_KW, A_VW, NA = 0, 1024, 1152, 1280, 1408, 1536
B_MQ, B_MV, B_MO, B_KC, B_VC, B_G0, B_IF, NB = 0, 1024, 1536, 2048, 2176, 2304, 2560, 2688


def _dot(a, b):
    return jnp.dot(a, b, preferred_element_type=f32)


def _rms(x, g):
    return x * lax.rsqrt(jnp.mean(x * x, axis=-1, keepdims=True) + EPS) * g


def _gelu(x):
    return 0.5 * x * (1.0 + lax.erf(x * 0.7071067811865476))


def _params(*sem):
    return pltpu.CompilerParams(dimension_semantics=sem, vmem_limit_bytes=VMEM_LIMIT)


def _const_spec(shape):
    nd = len(shape)
    return pl.BlockSpec(shape, lambda *_: (0,) * nd, pipeline_mode=pl.Buffered(1))


def _split_keys_t(ref, k, extra=None):
    kt = k.T
    n_extra = 0 if extra is None else extra.shape[0]
    for g in range(NSA_GROUPS):
        ref[g, 0:NSA_DH, :] = kt[g * NSA_DH:(g + 1) * NSA_DH].astype(bf16)
        if extra is not None:
            ref[g, NSA_DH:NSA_DH + n_extra, :] = extra
        ref[g, NSA_DH + n_extra:, :] = jnp.zeros((LANE - NSA_DH - n_extra, kt.shape[1]), bf16)


def _split_values(ref, v, ones):
    lane = lax.broadcasted_iota(jnp.int32, v.shape, 1)
    fill = jnp.where(lane == NSA_DH, 1.0, 0.0) if ones == "col" else 0.0
    for g in range(NSA_GROUPS):
        vg = v if g == 0 else pltpu.roll(v, LANE - g * NSA_DH, 1)
        ref[g, :, 0:LANE] = jnp.where(lane < NSA_DH, vg, fill).astype(bf16)
        if ones == "tile":
            ref[g, :, LANE:2 * LANE] = jnp.ones(v.shape, bf16)


def _inproj_kernel(x_ref, g_ref, w_ref, q_ref, kst_ref, vs_ref, kwt_ref, vw_ref, pb_ref, *, tm, nblk):
    h = _rms(x_ref[...], g_ref[...]).astype(bf16)
    q_ref[...] = (_dot(h, w_ref[:, A_Q:A_KS]) * NSA_QSCALE).astype(bf16)
    kv = _dot(h, w_ref[:, A_KS:NA])
    key = pl.program_id(0) * tm + lax.broadcasted_iota(jnp.int32, (nblk, tm), 1)
    rowj = lax.broadcasted_iota(jnp.int32, (nblk, tm), 0)
    ind = ((lax.shift_right_logical(key, 6) & (nblk - 1)) == rowj).astype(bf16)
    _split_keys_t(kst_ref, kv[:, 0:LANE], ind)
    _split_values(vs_ref, kv[:, LANE:2 * LANE], "col")
    _split_keys_t(kwt_ref, kv[:, 2 * LANE:3 * LANE])
    _split_values(vw_ref, kv[:, 3 * LANE:4 * LANE], "tile")
    for c0 in range(0, NB, 512):
        c1 = min(c0 + 512, NB)
        pb_ref[:, c0:c1] = _dot(h, w_ref[:, NA + c0:NA + c1])


def _inproj(x, g, w, nblk, tm=512):
    s = x.shape[0]
    assert nblk & (nblk - 1) == 0 and SEL_BLOCK == 64
    kspec = pl.BlockSpec((NSA_GROUPS, LANE, tm), lambda i: (0, 0, i))
    vspec = lambda w: pl.BlockSpec((NSA_GROUPS, tm, w), lambda i: (0, i, 0))
    kshape = jax.ShapeDtypeStruct((NSA_GROUPS, LANE, s), bf16)
    vshape = lambda w: jax.ShapeDtypeStruct((NSA_GROUPS, s, w), bf16)
    return pl.pallas_call(
        functools.partial(_inproj_kernel, tm=tm, nblk=nblk),
        grid=(s // tm,),
        in_specs=[pl.BlockSpec((tm, D_MODEL), lambda i: (i, 0)),
                  _const_spec((1, D_MODEL)),
                  _const_spec((D_MODEL, NA + NB))],
        out_specs=[pl.BlockSpec((tm, NSA_HEADS * LANE), lambda i: (i, 0)), kspec, vspec(LANE), kspec, vspec(2 * LANE),
                   pl.BlockSpec((tm, NB), lambda i: (i, 0))],
        out_shape=[jax.ShapeDtypeStruct((s, NSA_HEADS * LANE), bf16), kshape, vshape(LANE), kshape, vshape(2 * LANE),
                   jax.ShapeDtypeStruct((s, NB), f32)],
        compiler_params=_params("parallel"),
        name="inproj",
    )(x, g, w)


def _compress_kernel(kc_ref, vc_ref, pos_ref, w1_ref, w2_ref, kct_ref, vcp_ref):
    n = kc_ref.shape[0] // CMP_STRIDE

    def mlp(x_ref, a):
        top = bot = None
        for l in range(CMP_STRIDE):
            xl = x_ref[pl.ds(l, n, stride=CMP_STRIDE), :]
            t = _dot((xl + pos_ref[a, l:l + 1, :]).astype(bf16), w1_ref[a, l])
            b = _dot((xl + pos_ref[a, CMP_STRIDE + l:CMP_STRIDE + l + 1, :]).astype(bf16), w1_ref[a, CMP_STRIDE + l])
            top = t if top is None else top + t
            bot = b if bot is None else bot + b
        pre = top + pltpu.roll(bot, n - 1, 0)
        return _dot(_gelu(pre).astype(bf16), w2_ref[a])

    _split_keys_t(kct_ref, mlp(kc_ref, 0))
    _split_values(vcp_ref, mlp(vc_ref, 1), None)


def _compress(pb, pos, w1, w2):
    s = pb.shape[0]
    n = s // CMP_STRIDE
    return pl.pallas_call(
        _compress_kernel,
        grid=(1,),
        in_specs=[pl.BlockSpec((s, LANE), lambda i: (0, B_KC // LANE)),
                  pl.BlockSpec((s, LANE), lambda i: (0, B_VC // LANE)),
                  _const_spec(pos.shape), _const_spec(w1.shape), _const_spec(w2.shape)],
        out_specs=[pl.BlockSpec((NSA_GROUPS, LANE, n), lambda i: (0, 0, 0)),
                   pl.BlockSpec((NSA_GROUPS, n, LANE), lambda i: (0, 0, 0))],
        out_shape=[jax.ShapeDtypeStruct((NSA_GROUPS, LANE, n), bf16),
                   jax.ShapeDtypeStruct((NSA_GROUPS, n, LANE), bf16)],
        compiler_params=_params("arbitrary"),
        name="compress",
    )(pb, pb, pos, w1, w2)


def _cmp_kernel(q_ref, kct_ref, vc_ref, ov_ref, oc_ref, sel_ref, *, tq, steps, classes):
    nsel = ov_ref.shape[1]
    t0 = pl.program_id(0) * tq
    tpos = t0 + lax.broadcasted_iota(jnp.int32, (tq, 1), 0)
    any_vis = tpos >= CMP_LEN - 1
    cur = lax.shift_right_logical(tpos, 6)

    def body(ncmp, nsel_w):
        cend = lax.broadcasted_iota(jnp.int32, (1, ncmp), 1) * CMP_STRIDE + (CMP_LEN - 1)
        vis = cend <= tpos
        blk = lax.broadcasted_iota(jnp.int32, (1, nsel_w), 1)
        forced = (blk == 0) | (blk == cur) | (blk == cur - 1)
        scores = []
        qk = lambda h: _dot(q_ref[:, h * LANE:(h + 1) * LANE], kct_ref[h // NSA_HPG, :, 0:ncmp])
        ahead = 2
        pending = [qk(h) for h in range(ahead)]
        for g in range(NSA_GROUPS):
            psum = None
            for hh in range(NSA_HPG):
                h = g * NSA_HPG + hh
                c0 = h * LANE
                s = jnp.where(vis, pending.pop(0), NEG)
                if h + ahead < NSA_HEADS:
                    pending.append(qk(h + ahead))
                e = jnp.exp2(s - jnp.max(s, axis=-1, keepdims=True))
                l = jnp.sum(e, axis=-1, keepdims=True)
                p = e * jnp.where(any_vis, 1.0 / l, 0.0)
                oc_ref[:, c0:c0 + LANE] = _dot(p.astype(bf16), vc_ref[g, 0:ncmp, :])
                psum = p if psum is None else psum + p
            imp = _dot(psum.astype(bf16), ov_ref[0:ncmp, 0:nsel_w])
            scores.append(jnp.where(forced, -jnp.inf, jnp.where(blk <= cur, imp, -FORCE_BONUS)))
        score = jnp.concatenate(scores, axis=0).T
        blkt = lax.broadcasted_iota(jnp.int32, score.shape, 0).astype(f32)
        for _ in range(SEL_TOPK - 3):
            mx = jnp.max(score, axis=0, keepdims=True)
            j = jnp.min(jnp.where(score == mx, blkt, float(nsel_w)), axis=0, keepdims=True)
            score = jnp.where(blkt == j, -jnp.inf, score)
        selm = (score == -jnp.inf).astype(f32).T
        for g in range(NSA_GROUPS):
            sel_ref[g, :, 0:nsel_w] = selm[g * tq:(g + 1) * tq]
            if nsel_w < nsel:
                sel_ref[g, :, nsel_w:] = jnp.zeros((tq, nsel - nsel_w), f32)

    per = steps // classes
    for k in range(classes):
        @pl.when((pl.program_id(0) >= k * per) & (pl.program_id(0) < (k + 1) * per))
        def _(k=k):
            body(kct_ref.shape[2] * (k + 1) // classes, nsel * (k + 1) // classes)


def _cmp_attention(pa, kct, vc, ov, tq=512, classes=4):
    s = pa.shape[0]
    ncmp, nsel = ov.shape
    steps = s // tq
    assert steps % classes == 0 and (ncmp // classes) % LANE == 0 and (nsel // classes) % SUBLANE == 0
    return pl.pallas_call(
        functools.partial(_cmp_kernel, tq=tq, steps=steps, classes=classes),
        grid=(steps,),
        in_specs=[pl.BlockSpec((tq, NSA_HEADS * LANE), lambda i: (i, 0)),
                  _const_spec((NSA_GROUPS, LANE, ncmp)),
                  _const_spec((NSA_GROUPS, ncmp, LANE)),
                  _const_spec((ncmp, nsel))],
        out_specs=[pl.BlockSpec((tq, NSA_HEADS * LANE), lambda i: (i, 0)),
                   pl.BlockSpec((NSA_GROUPS, tq, nsel), lambda i: (0, i, 0))],
        out_shape=[jax.ShapeDtypeStruct((s, NSA_HEADS * LANE), f32),
                   jax.ShapeDtypeStruct((NSA_GROUPS, s, nsel), f32)],
        compiler_params=_params("parallel"),
        name="cmp_attn",
    )(pa, kct, vc, ov)


def _selwin_kernel(q_ref, oc_ref, sel_ref, gate_ref, gh_ref, kst_ref, vs_ref, kwt_ref, vw_ref, out_ref, *, tq, tk):
    nsel = sel_ref.shape[2]
    nblk = tk // SEL_BLOCK
    t0 = pl.program_id(1) * tq
    tpos = t0 + lax.broadcasted_iota(jnp.int32, (tq, 1), 0)
    q4 = [q_ref[:, hh * LANE:(hh + 1) * LANE] for hh in range(NSA_HPG)]
    sel = sel_ref[0]
    lane = lax.broadcasted_iota(jnp.int32, (tq, LANE), 1)
    bias_lanes = (lane >= NSA_DH) & (lane < NSA_DH + nblk)

    def q_biased(kt):
        sh = lax.rem(NSA_DH + nsel - kt * nblk, jnp.int32(nsel))
        bias = ((pltpu.roll(sel, sh, 1)[:, :LANE] - 1.0) * -NEG).astype(bf16)
        return [jnp.where(bias_lanes, bias, qh) for qh in q4]

    def attend(qb, st, width, carry, causal):
        kblk = kst_ref[0, :, pl.ds(st, width)]
        vblk = vs_ref[0, pl.ds(st, width), :]
        if causal:
            keep = st + lax.broadcasted_iota(jnp.int32, (1, width), 1) <= tpos
        out = []
        s_next = _dot(qb[0], kblk)
        for hh in range(NSA_HPG):
            m, acc = carry[hh]
            s = s_next
            if hh + 1 < NSA_HPG:
                s_next = _dot(qb[hh + 1], kblk)
            if causal:
                s = jnp.where(keep, s, NEG)
            mn = jnp.maximum(m, jnp.max(s, axis=-1, keepdims=True))
            p = jnp.exp2(s - mn)
            out.append((mn, jnp.exp2(m - mn) * acc + _dot(p.astype(bf16), vblk)))
        return tuple(out)

    last = t0 // tk
    init = tuple((jnp.full((tq, 1), NEG, f32), jnp.zeros((tq, LANE), f32)) for _ in range(NSA_HPG))
    carry = lax.fori_loop(
        0, last, lambda kt, c: attend(q_biased(kt), pl.multiple_of(kt * tk, tk), tk, c, False), init)
    qb_last = q_biased(last)
    base = last * tk
    carry = lax.fori_loop(
        0, (t0 - base) // tq,
        lambda j, c: attend(qb_last, pl.multiple_of(base + j * tq, tq), tq, c, False), carry)

    wlen = tq + WINDOW
    ws = pl.multiple_of(jnp.maximum(t0 - WINDOW, 0), LANE)
    kdiag = kst_ref[0, :, pl.ds(pl.multiple_of(t0, tq), tq)]
    vdiag = vs_ref[0, pl.ds(pl.multiple_of(t0, tq), tq), :]
    kwin = kwt_ref[0, :, pl.ds(ws, wlen)]
    vwin = vw_ref[0, pl.ds(ws, wlen), :]
    dmask = t0 + lax.broadcasted_iota(jnp.int32, (1, tq), 1) <= tpos
    kpos = ws + lax.broadcasted_iota(jnp.int32, (1, wlen), 1)
    wmask = (kpos <= tpos) & (kpos > tpos - WINDOW)
    chains = [(br, hh) for hh in range(NSA_HPG) for br in ("diag", "win")]
    qk = lambda br, hh: _dot(qb_last[hh], kdiag) if br == "diag" else _dot(q4[hh], kwin)

    ngate = 3 * NSA_HPG
    erow = lax.broadcasted_iota(jnp.int32, (LANE, ngate * LANE), 0)
    ecol = lax.broadcasted_iota(jnp.int32, (LANE, ngate * LANE), 1)
    expand = (erow == lax.shift_right_logical(ecol, 7)).astype(bf16)
    rest = jax.nn.sigmoid(gate_ref[...])
    gates = None
    for _ in range(3):
        term = rest.astype(bf16)
        rest = rest - term.astype(f32)
        part = _dot(term, expand)
        gates = part if gates is None else gates + part
    gate = lambda hh, j: gates[:, (3 * hh + j) * LANE:(3 * hh + j + 1) * LANE]

    valid = lane < NSA_DH
    ahead = 2
    pending = [qk(*c) for c in chains[:ahead]]
    acc_s = None
    for n, (br, hh) in enumerate(chains):
        s = pending.pop(0)
        if n + ahead < len(chains):
            pending.append(qk(*chains[n + ahead]))
        if br == "diag":
            m, acc = carry[hh]
            s = jnp.where(dmask, s, NEG)
            mn = jnp.maximum(m, jnp.max(s, axis=-1, keepdims=True))
            acc_s = jnp.exp2(m - mn) * acc + _dot(jnp.exp2(s - mn).astype(bf16), vdiag)
            continue
        s = jnp.where(wmask, s, NEG)
        pw = jnp.exp2(s - jnp.max(s, axis=-1, keepdims=True))
        acc_w = _dot(pw.astype(bf16), vwin)
        o_s = acc_s * (gate(hh, 1) / acc_s[:, NSA_DH:NSA_DH + 1])
        o_w = acc_w[:, :LANE] * (gate(hh, 2) / acc_w[:, LANE:])
        o = gate(hh, 0) * oc_ref[:, hh * LANE:(hh + 1) * LANE] + o_s + o_w
        o = jnp.where(valid, o, 0.0)
        ms = jnp.sum(o * o, axis=-1, keepdims=True) * (1.0 / NSA_DH)
        y = o * lax.rsqrt(ms + EPS) * gh_ref[:, hh * LANE:(hh + 1) * LANE]
        out_ref[:, hh * LANE:(hh + 1) * LANE] = y.astype(bf16)


def _selwin_attention(pa, pb, oc, sel, gh_pad, kst, vs, kwt, vw, tq=512, tk=2048):
    s = pa.shape[0]
    nsel = sel.shape[2]
    gw = NSA_HPG * LANE
    assert tk % tq == 0 and s % tk == 0
    kspec = pl.BlockSpec((1, LANE, s), lambda g, i: (g, 0, 0), pipeline_mode=pl.Buffered(1))
    vspec = lambda w: pl.BlockSpec((1, s, w), lambda g, i: (g, 0, 0), pipeline_mode=pl.Buffered(1))
    return pl.pallas_call(
        functools.partial(_selwin_kernel, tq=tq, tk=tk),
        grid=(NSA_GROUPS, s // tq),
        in_specs=[pl.BlockSpec((tq, gw), lambda g, i: (i, g)),
                  pl.BlockSpec((tq, gw), lambda g, i: (i, g)),
                  pl.BlockSpec((1, tq, nsel), lambda g, i: (g, i, 0)),
                  pl.BlockSpec((tq, LANE), lambda g, i: (i, B_G0 // LANE + g)),
                  pl.BlockSpec((1, gw), lambda g, i: (0, g)),
                  kspec, vspec(LANE), kspec, vspec(2 * LANE)],
        out_specs=pl.BlockSpec((tq, gw), lambda g, i: (i, g)),
        out_shape=jax.ShapeDtypeStruct((s, NSA_HEADS * LANE), bf16),
        compiler_params=_params("parallel", "parallel"),
        name="selwin_attn",
    )(pa, oc, sel, pb, gh_pad, kst, vs, kwt, vw)


def _mlstm_kernel(qk_ref, v_ref, og_ref, if_ref, cw_ref, bias_ref, gh_ref, out_ref, buf, cst, mst, *, tm):
    @pl.when(pl.program_id(0) == 0)
    def _():
        buf[0:SUBLANE, :] = jnp.zeros((SUBLANE, 2 * M_WIDTH), f32)
        cst[...] = jnp.zeros(cst.shape, f32)
        mst[...] = jnp.zeros(mst.shape, f32)

    qk = qk_ref[...]
    buf[SUBLANE:SUBLANE + tm, :] = qk
    conv = cw_ref[M_CONV - 1:M_CONV, :] * qk
    for k in range(M_CONV - 1):
        off = SUBLANE - (M_CONV - 1) + k
        conv = conv + cw_ref[k:k + 1, :] * buf[off:off + tm, :]
    buf[0:SUBLANE, :] = qk[tm - SUBLANE:tm]
    qkc = conv * jax.nn.sigmoid(conv)
    qs = (qkc[:, :M_WIDTH] * (M_DH ** -0.5)).astype(bf16)
    ks = qkc[:, M_WIDTH:]
    vb = v_ref[...].astype(bf16)
    ogate = jax.nn.sigmoid(og_ref[...])

    pre = if_ref[...] + bias_ref[...]
    lane = lax.broadcasted_iota(jnp.int32, (M_CHUNK, LANE), 1)
    rowi = lax.broadcasted_iota(jnp.int32, (M_CHUNK, LANE), 0)
    logf = jnp.minimum(pre, 0.0) - jnp.log1p(jnp.exp(-jnp.abs(pre)))
    tri = (lax.broadcasted_iota(jnp.int32, (M_CHUNK, M_CHUNK), 0)
           >= lax.broadcasted_iota(jnp.int32, (M_CHUNK, M_CHUNK), 1))
    ones_col = (lax.broadcasted_iota(jnp.int32, (M_CHUNK, LANE), 1) == 0).astype(bf16)

    tn = (((1,), (1,)), ((), ()))
    tt = (((0,), (0,)), ((), ()))
    for c in range(tm // M_CHUNK):
        r0, r1 = c * M_CHUNK, (c + 1) * M_CHUNK
        cum = logf[r0:r1]
        sft = 1
        while sft < M_CHUNK:
            cum = cum + jnp.where(rowi >= sft, pltpu.roll(cum, sft, 0), 0.0)
            sft *= 2
        comb = jnp.where(lane < M_HEADS, pre[r0:r1], cum)
        comb_t = comb.T
        part = []
        for h in range(M_HEADS):
            hs = slice(h * M_DH, (h + 1) * M_DH)
            bcol = comb[:, M_HEADS + h:M_HEADS + h + 1]
            icol = comb[:, h:h + 1]
            mprev = mst[0:1, h:h + 1]
            qh = qs[r0:r1, hs]
            kh = ks[r0:r1, hs]
            v_aug = jnp.concatenate([vb[r0:r1, hs], ones_col], axis=1)
            qk_s = lax.dot_general(qh, kh.astype(bf16), tn, preferred_element_type=f32)
            inter = _dot(qh, cst[h].astype(bf16))
            b_last = bcol[M_CHUNK - 1:M_CHUNK, :]
            g_s = b_last - bcol + icol
            m_new = jnp.maximum(b_last + mprev, jnp.max(g_s, axis=0, keepdims=True))
            w_s = jnp.exp(g_s - m_new)
            decay = jnp.exp(b_last + mprev - m_new)
            upd = lax.dot_general((kh * w_s).astype(bf16), v_aug, tt, preferred_element_type=f32)
            cst[h] = decay * cst[h] + upd
            mst[0:1, h:h + 1] = m_new
            part.append((bcol, mprev, qk_s, inter, v_aug))
        for h in range(M_HEADS):
            hs = slice(h * M_DH, (h + 1) * M_DH)
            bcol, mprev, qk_s, inter, v_aug = part[h]
            brow = comb_t[M_HEADS + h:M_HEADS + h + 1, :]
            irow = comb_t[h:h + 1, :]
            dmat = jnp.where(tri, bcol - brow + irow, -jnp.inf)
            m_inter = bcol + mprev
            m_t = jnp.maximum(jnp.max(dmat, axis=-1, keepdims=True), m_inter)
            sc = qk_s * jnp.exp(dmat - m_t)
            tot = jnp.exp(m_inter - m_t) * inter + _dot(sc.astype(bf16), v_aug)
            den = tot[:, M_DH:M_DH + 1]
            hout = tot[:, :M_DH] / jnp.maximum(jnp.abs(den), jnp.exp(-m_t))
            hn = hout * lax.rsqrt(jnp.mean(hout * hout, axis=-1, keepdims=True) + EPS) * gh_ref[:, hs]
            out_ref[r0:r1, hs] = (ogate[r0:r1, hs] * hn).astype(bf16)


def _mlstm(pb, conv_w, bias_if, g_head, tm=256):
    s = pb.shape[0]
    return pl.pallas_call(
        functools.partial(_mlstm_kernel, tm=tm),
        grid=(s // tm,),
        in_specs=[pl.BlockSpec((tm, 2 * M_WIDTH), lambda i: (i, B_MQ // (2 * M_WIDTH))),
                  pl.BlockSpec((tm, M_WIDTH), lambda i: (i, B_MV // M_WIDTH)),
                  pl.BlockSpec((tm, M_WIDTH), lambda i: (i, B_MO // M_WIDTH)),
                  pl.BlockSpec((tm, LANE), lambda i: (i, B_IF // LANE)),
                  _const_spec((M_CONV, 2 * M_WIDTH)),
                  _const_spec((1, LANE)),
                  _const_spec((1, M_WIDTH))],
        out_specs=pl.BlockSpec((tm, M_WIDTH), lambda i: (i, 0)),
        out_shape=jax.ShapeDtypeStruct((s, M_WIDTH), bf16),
        scratch_shapes=[pltpu.VMEM((tm + SUBLANE, 2 * M_WIDTH), f32),
                        pltpu.VMEM((M_HEADS, M_DH, 2 * M_DH), f32),
                        pltpu.VMEM((SUBLANE, LANE), f32)],
        compiler_params=_params("arbitrary"),
        name="mlstm",
    )(pb, pb, pb, pb, conv_w, bias_if, g_head)


def _memkv_kernel(mem_ref, g_ref, wk_ref, wv_ref, k_ref, v_ref):
    hm = _rms(mem_ref[...], g_ref[...]).astype(bf16)
    k_ref[...] = _dot(hm, wk_ref[...]).astype(bf16)
    v_ref[...] = _dot(hm, wv_ref[...]).astype(bf16)


def _memkv(mem, g, wk, wv):
    m = mem.shape[0]
    return pl.pallas_call(
        _memkv_kernel,
        out_shape=[jax.ShapeDtypeStruct((m, D_MODEL), bf16)] * 2,
        compiler_params=pltpu.CompilerParams(vmem_limit_bytes=VMEM_LIMIT),
        name="mem_kv",
    )(mem, g, wk, wv)


def _mix_mem_kernel(x_ref, nsa_ref, ml_ref, won_ref, wom_ref, gq_ref, wq_ref, kt_ref, v_ref, wo_ref, o_ref):
    x1 = x_ref[...] + _dot(nsa_ref[...], won_ref[...]) + _dot(ml_ref[...], wom_ref[...])
    hq = _rms(x1, gq_ref[...]).astype(bf16)
    q = (_dot(hq, wq_ref[...]) * (MEM_DH ** -0.5)).astype(bf16)
    outs = []
    heads = [slice(h * MEM_DH, (h + 1) * MEM_DH) for h in range(MEM_HEADS)]
    scores = [_dot(q[:, hs], kt_ref[hs, :]) for hs in heads]
    for hs, s in zip(heads, scores):
        e = jnp.exp(s - jnp.max(s, axis=-1, keepdims=True))
        p = e / jnp.sum(e, axis=-1, keepdims=True)
        outs.append(_dot(p.astype(bf16), v_ref[:, hs]).astype(bf16))
    o_ref[...] = x1 + _dot(jnp.concatenate(outs, axis=1), wo_ref[...])


def _mix_mem(x, nsa_n, ml, w_out_nsa, w_out_m, g_q, w_q, k_t, v, w_o, tm=512):
    s = x.shape[0]
    m = v.shape[0]
    return pl.pallas_call(
        _mix_mem_kernel,
        grid=(s // tm,),
        in_specs=[pl.BlockSpec((tm, D_MODEL), lambda i: (i, 0)),
                  pl.BlockSpec((tm, NSA_HEADS * LANE), lambda i: (i, 0)),
                  pl.BlockSpec((tm, M_WIDTH), lambda i: (i, 0)),
                  _const_spec((NSA_HEADS * LANE, D_MODEL)),
                  _const_spec((M_WIDTH, D_MODEL)),
                  _const_spec((1, D_MODEL)),
                  _const_spec((D_MODEL, D_MODEL)),
                  _const_spec((D_MODEL, m)),
                  _const_spec((m, D_MODEL)),
                  _const_spec((D_MODEL, D_MODEL))],
        out_specs=pl.BlockSpec((tm, D_MODEL), lambda i: (i, 0)),
        out_shape=jax.ShapeDtypeStruct((s, D_MODEL), f32),
        compiler_params=_params("parallel"),
        name="mix_mem",
    )(x, nsa_n, ml, w_out_nsa, w_out_m, g_q, w_q, k_t, v, w_o)


def _ffn_kernel(x_ref, xh_ref, g_ref, wup_ref, cw_ref, wdn_ref, gf_ref, o_ref, *, tm, widths):
    x = x_ref[...]
    hf = _rms(x, g_ref[...]).astype(bf16)
    hist = jnp.where(pl.program_id(0) > 0, 1.0, 0.0)
    hh = (_rms(xh_ref[...], g_ref[...]) * hist).astype(bf16)

    def conv_cols(c0, cw):
        w = wup_ref[:, c0:c0 + cw]
        u = _dot(hf, w)
        uh = _dot(hh, w)
        row = lax.broadcasted_iota(jnp.int32, (tm, cw), 0)
        y = cw_ref[F_CONV - 1:F_CONV, c0:c0 + cw] * u
        for k in range(F_CONV - 1):
            d = F_CONV - 1 - k
            sh = pltpu.roll(u, d, 0)
            for r in range(d):
                sh = jnp.where(row == r, uh[SUBLANE - d + r:SUBLANE - d + r + 1, :], sh)
            y = y + cw_ref[k:k + 1, c0:c0 + cw] * sh
        return y

    starts = [sum(widths[:n]) for n in range(len(widths))]
    both = lambda n: (conv_cols(starts[n], widths[n]), conv_cols(D_FF + starts[n], widths[n]))
    acc = x
    nxt = both(0)
    for n, (c0, cw) in enumerate(zip(starts, widths)):
        ya, yb = nxt
        if n + 1 < len(widths):
            nxt = both(n + 1)
        gated = _gelu(ya) * yb
        acc = acc + _dot(gated.astype(bf16), wdn_ref[c0:c0 + cw, :])
    o_ref[...] = _rms(acc, gf_ref[...])


def _ffn(x, g, w_up, conv_w, w_down, g_final, tm=1024, widths=(1280, 1536)):
    s = x.shape[0]
    halo = tm // SUBLANE
    assert sum(widths) == D_FF
    return pl.pallas_call(
        functools.partial(_ffn_kernel, tm=tm, widths=widths),
        grid=(s // tm,),
        in_specs=[pl.BlockSpec((tm, D_MODEL), lambda i: (i, 0)),
                  pl.BlockSpec((SUBLANE, D_MODEL), lambda i: (jnp.maximum(i * halo - 1, 0), 0)),
                  _const_spec((1, D_MODEL)),
                  _const_spec((D_MODEL, 2 * D_FF)),
                  _const_spec((F_CONV, 2 * D_FF)),
                  _const_spec((D_FF, D_MODEL)),
                  _const_spec((1, D_MODEL))],
        out_specs=pl.BlockSpec((tm, D_MODEL), lambda i: (i, 0)),
        out_shape=jax.ShapeDtypeStruct((s, D_MODEL), f32),
        compiler_params=_params("parallel"),
        name="ffn",
    )(x, x, g, w_up, conv_w, w_down, g_final)


def _pad_cols(a, n):
    return jnp.pad(a, ((0, 0), (0, n - a.shape[1])))


def _pad_heads(a, heads, dh):
    r = a.shape[0]
    return jnp.pad(a.reshape(r, heads, dh), ((0, 0), (0, 0), (0, LANE - dh))).reshape(r, heads * LANE)


def _build_w_in(w):
    o = 0
    parts = {}
    sizes = (("q", 512), ("kc", 128), ("vc", 128), ("ks", 128), ("vs", 128), ("kw", 128), ("vw", 128),
             ("gt", 24), ("mq", 512), ("mk", 512), ("mv", 512), ("mo", 512), ("mi", 4), ("mf", 4))
    for name, n in sizes:
        parts[name] = w[:, o:o + n]
        o += n
    gt = parts["gt"].reshape(D_MODEL, NSA_GROUPS, NSA_HPG * 3)
    cols = [_pad_heads(parts["q"], NSA_HEADS, NSA_DH), parts["ks"], parts["vs"], parts["kw"], parts["vw"],
            parts["mq"], parts["mk"], parts["mv"], parts["mo"], parts["kc"], parts["vc"],
            _pad_cols(gt[:, 0], LANE), _pad_cols(gt[:, 1], LANE),
            _pad_cols(jnp.concatenate([parts["mi"], parts["mf"]], axis=1), LANE)]
    return jnp.concatenate(cols, axis=1).astype(bf16)


def _block_diag2(w):
    z = jnp.zeros_like(w)
    return jnp.concatenate([jnp.concatenate([w, z], axis=-1), jnp.concatenate([z, w], axis=-1)], axis=-2)


def kernel(x, mem, g_mix, w_in, cmp_pos_k, cmp_w1_k, cmp_w2_k, cmp_pos_v, cmp_w1_v, cmp_w2_v, mlstm_conv_w, mlstm_b_i, mlstm_b_f, g_head_nsa, g_head_mlstm, w_out, g_mem_q, g_mem_kv, w_mem_q, w_mem_k, w_mem_v, w_mem_o, g_ffn, w_up, ffn_conv_w, w_down, g_final):
    b, s, _ = x.shape
    assert b == 1 and g_mix.shape[0] == 1
    sel_tk = 2048
    assert s % 8192 == 0 and s >= sel_tk + WINDOW
    l = 0
    xs = x[0]
    row = lambda a: a.reshape(1, -1)

    qa, kst, vs, kwt, vw, pb = _inproj(xs, row(g_mix[l]), _build_w_in(w_in[l]), sel_tk // SEL_BLOCK)

    n16 = s // CMP_STRIDE
    dup = lambda p: jnp.concatenate([p, p], axis=-1)
    pos = jnp.stack([dup(cmp_pos_k[l]), dup(cmp_pos_v[l])])
    w1 = _block_diag2(jnp.stack([cmp_w1_k[l], cmp_w1_v[l]]).reshape(2, CMP_LEN, NSA_DH, -1)).astype(bf16)
    w2 = _block_diag2(jnp.stack([cmp_w2_k[l], cmp_w2_v[l]])).astype(bf16)
    kct, vc = _compress(pb, pos, w1, w2)

    n_sel = s // SEL_BLOCK
    cstart = jnp.arange(n16)[:, None] * CMP_STRIDE
    sstart = jnp.arange(n_sel)[None, :] * SEL_BLOCK
    overlap = ((cstart < sstart + SEL_BLOCK) & (cstart + CMP_LEN > sstart)).astype(bf16)
    oc, sel = _cmp_attention(qa, kct, vc, overlap)

    gh_pad = _pad_heads(row(g_head_nsa[l]), NSA_HEADS, NSA_DH)
    nsa_n = _selwin_attention(qa, pb, oc, sel, gh_pad, kst, vs, kwt, vw, tk=sel_tk)

    bias_if = _pad_cols(jnp.concatenate([row(mlstm_b_i[l]), row(mlstm_b_f[l])], axis=1), LANE)
    ml = _mlstm(pb, mlstm_conv_w[l], bias_if, row(g_head_mlstm[l]))

    w_out_nsa = jnp.pad(w_out[l][:NSA_HEADS * NSA_DH].reshape(NSA_HEADS, NSA_DH, D_MODEL),
                        ((0, 0), (0, LANE - NSA_DH), (0, 0))).reshape(NSA_HEADS * LANE, D_MODEL).astype(bf16)
    w_out_m = w_out[l][NSA_HEADS * NSA_DH:].astype(bf16)
    k_mem, v_mem = _memkv(mem[0], row(g_mem_kv[l]), w_mem_k[l].astype(bf16), w_mem_v[l].astype(bf16))
    x2 = _mix_mem(xs, nsa_n, ml, w_out_nsa, w_out_m, row(g_mem_q[l]), w_mem_q[l].astype(bf16),
                  k_mem.T, v_mem, w_mem_o[l].astype(bf16))

    out = _ffn(x2, row(g_ffn[l]), w_up[l].astype(bf16), ffn_conv_w[l], w_down[l].astype(bf16), row(g_final))
    return out[None]
```

```python
import functools

import jax
import jax.numpy as jnp
from jax import lax
from jax.experimental import pallas as pl
from jax.experimental.pallas import tpu as pltpu

f32 = jnp.float32
bf16 = jnp.bfloat16

D_MODEL = 1024
NSA_HEADS = 8
NSA_GROUPS = 2
NSA_HPG = NSA_HEADS // NSA_GROUPS
NSA_DH = 64
CMP_LEN = 32
CMP_STRIDE = 16
SEL_BLOCK = 64
SEL_TOPK = 16
WINDOW = 512
FORCE_BONUS = 1.0e4
M_HEADS = 4
M_DH = 128
M_WIDTH = M_HEADS * M_DH
M_CONV = 4
M_CHUNK = 64
MEM_HEADS = 4
MEM_DH = D_MODEL // MEM_HEADS
D_FF = 2816
F_CONV = 3
EPS = 1e-6
NSA_QSCALE = NSA_DH ** -0.5 * 1.4426950408889634

LANE = 128
SUBLANE = 8
NEG = -1e30
VMEM_LIMIT = 56 * 1024 * 1024

A_Q, A_KS, A_VS, A_KW, A_VW, NA = 0, 1024, 1152, 1280, 1408, 1536
B_MQ, B_MV, B_MO, B_KC, B_VC, B_G0, B_IF, NB = 0, 1024, 1536, 2048, 2176, 2304, 2560, 2688


def _dot(a, b):
    return jnp.dot(a, b, preferred_element_type=f32)


def _rms(x, g):
    return x * lax.rsqrt(jnp.mean(x * x, axis=-1, keepdims=True) + EPS) * g


def _gelu(x):
    return 0.5 * x * (1.0 + lax.erf(x * 0.7071067811865476))


def _params(*sem):
    return pltpu.CompilerParams(dimension_semantics=sem, vmem_limit_bytes=VMEM_LIMIT)


def _const_spec(shape):
    nd = len(shape)
    return pl.BlockSpec(shape, lambda *_: (0,) * nd, pipeline_mode=pl.Buffered(1))


def _split_keys_t(ref, k, extra=None):
    kt = k.T
    n_extra = 0 if extra is None else extra.shape[0]
    for g in range(NSA_GROUPS):
        ref[g, 0:NSA_DH, :] = kt[g * NSA_DH:(g + 1) * NSA_DH].astype(bf16)
        if extra is not None:
            ref[g, NSA_DH:NSA_DH + n_extra, :] = extra
        ref[g, NSA_DH + n_extra:, :] = jnp.zeros((LANE - NSA_DH - n_extra, kt.shape[1]), bf16)


def _split_values(ref, v, ones):
    lane = lax.broadcasted_iota(jnp.int32, v.shape, 1)
    fill = jnp.where(lane == NSA_DH, 1.0, 0.0) if ones == "col" else 0.0
    for g in range(NSA_GROUPS):
        vg = v if g == 0 else pltpu.roll(v, LANE - g * NSA_DH, 1)
        ref[g, :, 0:LANE] = jnp.where(lane < NSA_DH, vg, fill).astype(bf16)
        if ones == "tile":
            ref[g, :, LANE:2 * LANE] = jnp.ones(v.shape, bf16)


def _inproj_kernel(x_ref, g_ref, w_ref, q_ref, kst_ref, vs_ref, kwt_ref, vw_ref, pb_ref, *, tm, nblk):
    h = _rms(x_ref[...], g_ref[...]).astype(bf16)
    q_ref[...] = (_dot(h, w_ref[:, A_Q:A_KS]) * NSA_QSCALE).astype(bf16)
    kv = _dot(h, w_ref[:, A_KS:NA])
    key = pl.program_id(0) * tm + lax.broadcasted_iota(jnp.int32, (nblk, tm), 1)
    rowj = lax.broadcasted_iota(jnp.int32, (nblk, tm), 0)
    ind = ((lax.shift_right_logical(key, 6) & (nblk - 1)) == rowj).astype(bf16)
    _split_keys_t(kst_ref, kv[:, 0:LANE], ind)
    _split_values(vs_ref, kv[:, LANE:2 * LANE], "col")
    _split_keys_t(kwt_ref, kv[:, 2 * LANE:3 * LANE])
    _split_values(vw_ref, kv[:, 3 * LANE:4 * LANE], "tile")
    for c0 in range(0, NB, 512):
        c1 = min(c0 + 512, NB)
        pb_ref[:, c0:c1] = _dot(h, w_ref[:, NA + c0:NA + c1])


def _inproj(x, g, w, nblk, tm=512):
    s = x.shape[0]
    assert nblk & (nblk - 1) == 0 and SEL_BLOCK == 64
    kspec = pl.BlockSpec((NSA_GROUPS, LANE, tm), lambda i: (0, 0, i))
    vspec = lambda w: pl.BlockSpec((NSA_GROUPS, tm, w), lambda i: (0, i, 0))
    kshape = jax.ShapeDtypeStruct((NSA_GROUPS, LANE, s), bf16)
    vshape = lambda w: jax.ShapeDtypeStruct((NSA_GROUPS, s, w), bf16)
    return pl.pallas_call(
        functools.partial(_inproj_kernel, tm=tm, nblk=nblk),
        grid=(s // tm,),
        in_specs=[pl.BlockSpec((tm, D_MODEL), lambda i: (i, 0)),
                  _const_spec((1, D_MODEL)),
                  _const_spec((D_MODEL, NA + NB))],
        out_specs=[pl.BlockSpec((tm, NSA_HEADS * LANE), lambda i: (i, 0)), kspec, vspec(LANE), kspec, vspec(2 * LANE),
                   pl.BlockSpec((tm, NB), lambda i: (i, 0))],
        out_shape=[jax.ShapeDtypeStruct((s, NSA_HEADS * LANE), bf16), kshape, vshape(LANE), kshape, vshape(2 * LANE),
                   jax.ShapeDtypeStruct((s, NB), f32)],
        compiler_params=_params("parallel"),
        name="inproj",
    )(x, g, w)


def _compress_kernel(kc_ref, vc_ref, pos_ref, w1_ref, w2_ref, kct_ref, vcp_ref):
    n = kc_ref.shape[0] // CMP_STRIDE

    def mlp(x_ref, a):
        top = bot = None
        for l in range(CMP_STRIDE):
            xl = x_ref[pl.ds(l, n, stride=CMP_STRIDE), :]
            t = _dot((xl + pos_ref[a, l:l + 1, :]).astype(bf16), w1_ref[a, l])
            b = _dot((xl + pos_ref[a, CMP_STRIDE + l:CMP_STRIDE + l + 1, :]).astype(bf16), w1_ref[a, CMP_STRIDE + l])
            top = t if top is None else top + t
            bot = b if bot is None else bot + b
        pre = top + pltpu.roll(bot, n - 1, 0)
        return _dot(_gelu(pre).astype(bf16), w2_ref[a])

    _split_keys_t(kct_ref, mlp(kc_ref, 0))
    _split_values(vcp_ref, mlp(vc_ref, 1), None)


def _compress(pb, pos, w1, w2):
    s = pb.shape[0]
    n = s // CMP_STRIDE
    return pl.pallas_call(
        _compress_kernel,
        grid=(1,),
        in_specs=[pl.BlockSpec((s, LANE), lambda i: (0, B_KC // LANE)),
                  pl.BlockSpec((s, LANE), lambda i: (0, B_VC // LANE)),
                  _const_spec(pos.shape), _const_spec(w1.shape), _const_spec(w2.shape)],
        out_specs=[pl.BlockSpec((NSA_GROUPS, LANE, n), lambda i: (0, 0, 0)),
                   pl.BlockSpec((NSA_GROUPS, n, LANE), lambda i: (0, 0, 0))],
        out_shape=[jax.ShapeDtypeStruct((NSA_GROUPS, LANE, n), bf16),
                   jax.ShapeDtypeStruct((NSA_GROUPS, n, LANE), bf16)],
        compiler_params=_params("arbitrary"),
        name="compress",
    )(pb, pb, pos, w1, w2)


def _cmp_kernel(q_ref, kct_ref, vc_ref, ov_ref, oc_ref, sel_ref, *, tq, steps, classes):
    nsel = ov_ref.shape[1]
    t0 = pl.program_id(0) * tq
    tpos = t0 + lax.broadcasted_iota(jnp.int32, (tq, 1), 0)
    any_vis = tpos >= CMP_LEN - 1
    cur = lax.shift_right_logical(tpos, 6)

    def body(ncmp, nsel_w):
        cend = lax.broadcasted_iota(jnp.int32, (1, ncmp), 1) * CMP_STRIDE + (CMP_LEN - 1)
        vis = cend <= tpos
        blk = lax.broadcasted_iota(jnp.int32, (1, nsel_w), 1)
        forced = (blk == 0) | (blk == cur) | (blk == cur - 1)
        scores = []
        qk = lambda h: _dot(q_ref[:, h * LANE:(h + 1) * LANE], kct_ref[h // NSA_HPG, :, 0:ncmp])
        ahead = 2
        pending = [qk(h) for h in range(ahead)]
        for g in range(NSA_GROUPS):
            psum = None
            for hh in range(NSA_HPG):
                h = g * NSA_HPG + hh
                c0 = h * LANE
                s = jnp.where(vis, pending.pop(0), NEG)
                if h + ahead < NSA_HEADS:
                    pending.append(qk(h + ahead))
                e = jnp.exp2(s - jnp.max(s, axis=-1, keepdims=True))
                l = jnp.sum(e, axis=-1, keepdims=True)
                p = e * jnp.where(any_vis, 1.0 / l, 0.0)
                oc_ref[:, c0:c0 + LANE] = _dot(p.astype(bf16), vc_ref[g, 0:ncmp, :])
                psum = p if psum is None else psum + p
            imp = _dot(psum.astype(bf16), ov_ref[0:ncmp, 0:nsel_w])
            scores.append(jnp.where(forced, -jnp.inf, jnp.where(blk <= cur, imp, -FORCE_BONUS)))
        score = jnp.concatenate(scores, axis=0).T
        blkt = lax.broadcasted_iota(jnp.int32, score.shape, 0).astype(f32)
        for _ in range(SEL_TOPK - 3):
            mx = jnp.max(score, axis=0, keepdims=True)
            j = jnp.min(jnp.where(score == mx, blkt, float(nsel_w)), axis=0, keepdims=True)
            score = jnp.where(blkt == j, -jnp.inf, score)
        selm = (score == -jnp.inf).astype(f32).T
        for g in range(NSA_GROUPS):
            sel_ref[g, :, 0:nsel_w] = selm[g * tq:(g + 1) * tq]
            if nsel_w < nsel:
                sel_ref[g, :, nsel_w:] = jnp.zeros((tq, nsel - nsel_w), f32)

    per = steps // classes
    for k in range(classes):
        @pl.when((pl.program_id(0) >= k * per) & (pl.program_id(0) < (k + 1) * per))
        def _(k=k):
            body(kct_ref.shape[2] * (k + 1) // classes, nsel * (k + 1) // classes)


def _cmp_attention(pa, kct, vc, ov, tq=512, classes=4):
    s = pa.shape[0]
    ncmp, nsel = ov.shape
    steps = s // tq
    assert steps % classes == 0 and (ncmp // classes) % LANE == 0 and (nsel // classes) % SUBLANE == 0
    return pl.pallas_call(
        functools.partial(_cmp_kernel, tq=tq, steps=steps, classes=classes),
        grid=(steps,),
        in_specs=[pl.BlockSpec((tq, NSA_HEADS * LANE), lambda i: (i, 0)),
                  _const_spec((NSA_GROUPS, LANE, ncmp)),
                  _const_spec((NSA_GROUPS, ncmp, LANE)),
                  _const_spec((ncmp, nsel))],
        out_specs=[pl.BlockSpec((tq, NSA_HEADS * LANE), lambda i: (i, 0)),
                   pl.BlockSpec((NSA_GROUPS, tq, nsel), lambda i: (0, i, 0))],
        out_shape=[jax.ShapeDtypeStruct((s, NSA_HEADS * LANE), f32),
                   jax.ShapeDtypeStruct((NSA_GROUPS, s, nsel), f32)],
        compiler_params=_params("parallel"),
        name="cmp_attn",
    )(pa, kct, vc, ov)


def _selwin_kernel(q_ref, oc_ref, sel_ref, gate_ref, gh_ref, kst_ref, vs_ref, kwt_ref, vw_ref, out_ref, *, tq, tk):
    nsel = sel_ref.shape[2]
    nblk = tk // SEL_BLOCK
    t0 = pl.program_id(1) * tq
    tpos = t0 + lax.broadcasted_iota(jnp.int32, (tq, 1), 0)
    q4 = [q_ref[:, hh * LANE:(hh + 1) * LANE] for hh in range(NSA_HPG)]
    sel = sel_ref[0]
    lane = lax.broadcasted_iota(jnp.int32, (tq, LANE), 1)
    bias_lanes = (lane >= NSA_DH) & (lane < NSA_DH + nblk)

    def q_biased(kt):
        sh = lax.rem(NSA_DH + nsel - kt * nblk, jnp.int32(nsel))
        bias = ((pltpu.roll(sel, sh, 1)[:, :LANE] - 1.0) * -NEG).astype(bf16)
        return [jnp.where(bias_lanes, bias, qh) for qh in q4]

    def attend(qb, st, width, carry, causal):
        kblk = kst_ref[0, :, pl.ds(st, width)]
        vblk = vs_ref[0, pl.ds(st, width), :]
        if causal:
            keep = st + lax.broadcasted_iota(jnp.int32, (1, width), 1) <= tpos
        out = []
        s_next = _dot(qb[0], kblk)
        for hh in range(NSA_HPG):
            m, acc = carry[hh]
            s = s_next
            if hh + 1 < NSA_HPG:
                s_next = _dot(qb[hh + 1], kblk)
            if causal:
                s = jnp.where(keep, s, NEG)
            mn = jnp.maximum(m, jnp.max(s, axis=-1, keepdims=True))
            p = jnp.exp2(s - mn)
            out.append((mn, jnp.exp2(m - mn) * acc + _dot(p.astype(bf16), vblk)))
        return tuple(out)

    last = t0 // tk
    init = tuple((jnp.full((tq, 1), NEG, f32), jnp.zeros((tq, LANE), f32)) for _ in range(NSA_HPG))
    carry = lax.fori_loop(
        0, last, lambda kt, c: attend(q_biased(kt), pl.multiple_of(kt * tk, tk), tk, c, False), init)
    qb_last = q_biased(last)
    base = last * tk
    carry = lax.fori_loop(
        0, (t0 - base) // tq,
        lambda j, c: attend(qb_last, pl.multiple_of(base + j * tq, tq), tq, c, False), carry)

    parts = 4
    hq = tq // parts
    rows = lambda r: slice(r * hq, (r + 1) * hq)
    kdiag = kst_ref[0, :, pl.ds(pl.multiple_of(t0, tq), tq)]
    vdiag = vs_ref[0, pl.ds(pl.multiple_of(t0, tq), tq), :]
    dmask = t0 + lax.broadcasted_iota(jnp.int32, (1, tq), 1) <= tpos
    win = []
    for r in range(parts):
        tp = t0 + r * hq + lax.broadcasted_iota(jnp.int32, (hq, 1), 0)
        st = pl.multiple_of(jnp.maximum(t0 + r * hq - WINDOW, 0), LANE)
        kpos = st + lax.broadcasted_iota(jnp.int32, (1, hq + WINDOW), 1)
        win.append((kwt_ref[0, :, pl.ds(st, hq + WINDOW)], vw_ref[0, pl.ds(st, hq + WINDOW), :],
                    (kpos <= tp) & (kpos > tp - WINDOW)))
    chains = [c for hh in range(NSA_HPG) for c in [("diag", hh, 0)] + [("win", hh, r) for r in range(parts)]]
    qk = lambda br, hh, r: _dot(qb_last[hh], kdiag) if br == "diag" else _dot(q4[hh][rows(r)], win[r][0])

    ngate = 3 * NSA_HPG
    erow = lax.broadcasted_iota(jnp.int32, (LANE, ngate * LANE), 0)
    ecol = lax.broadcasted_iota(jnp.int32, (LANE, ngate * LANE), 1)
    expand = (erow == lax.shift_right_logical(ecol, 7)).astype(bf16)
    rest = jax.nn.sigmoid(gate_ref[...])
    gates = None
    for _ in range(3):
        term = rest.astype(bf16)
        rest = rest - term.astype(f32)
        part = _dot(term, expand)
        gates = part if gates is None else gates + part
    gate = lambda hh, j, r: gates[rows(r), (3 * hh + j) * LANE:(3 * hh + j + 1) * LANE]

    valid = lax.broadcasted_iota(jnp.int32, (hq, LANE), 1) < NSA_DH
    ahead = 2
    pending = [qk(*c) for c in chains[:ahead]]
    o_sel = None
    for n, (br, hh, r) in enumerate(chains):
        s = pending.pop(0)
        if n + ahead < len(chains):
            pending.append(qk(*chains[n + ahead]))
        if br == "diag":
            m, acc = carry[hh]
            s = jnp.where(dmask, s, NEG)
            mn = jnp.maximum(m, jnp.max(s, axis=-1, keepdims=True))
            acc_s = jnp.exp2(m - mn) * acc + _dot(jnp.exp2(s - mn).astype(bf16), vdiag)
            o_sel = acc_s / acc_s[:, NSA_DH:NSA_DH + 1]
            continue
        _, vwin, wmask = win[r]
        s = jnp.where(wmask, s, NEG)
        pw = jnp.exp2(s - jnp.max(s, axis=-1, keepdims=True))
        acc_w = _dot(pw.astype(bf16), vwin)
        o_w = acc_w[:, :LANE] * (gate(hh, 2, r) / acc_w[:, LANE:])
        o = gate(hh, 0, r) * oc_ref[rows(r), hh * LANE:(hh + 1) * LANE] + gate(hh, 1, r) * o_sel[rows(r)] + o_w
        o = jnp.where(valid, o, 0.0)
        ms = jnp.sum(o * o, axis=-1, keepdims=True) * (1.0 / NSA_DH)
        y = o * lax.rsqrt(ms + EPS) * gh_ref[:, hh * LANE:(hh + 1) * LANE]
        out_ref[rows(r), hh * LANE:(hh + 1) * LANE] = y.astype(bf16)


def _selwin_attention(pa, pb, oc, sel, gh_pad, kst, vs, kwt, vw, tq=512, tk=2048):
    s = pa.shape[0]
    nsel = sel.shape[2]
    gw = NSA_HPG * LANE
    assert tk % tq == 0 and s % tk == 0
    kspec = pl.BlockSpec((1, LANE, s), lambda g, i: (g, 0, 0), pipeline_mode=pl.Buffered(1))
    vspec = lambda w: pl.BlockSpec((1, s, w), lambda g, i: (g, 0, 0), pipeline_mode=pl.Buffered(1))
    return pl.pallas_call(
        functools.partial(_selwin_kernel, tq=tq, tk=tk),
        grid=(NSA_GROUPS, s // tq),
        in_specs=[pl.BlockSpec((tq, gw), lambda g, i: (i, g)),
                  pl.BlockSpec((tq, gw), lambda g, i: (i, g)),
                  pl.BlockSpec((1, tq, nsel), lambda g, i: (g, i, 0)),
                  pl.BlockSpec((tq, LANE), lambda g, i: (i, B_G0 // LANE + g)),
                  pl.BlockSpec((1, gw), lambda g, i: (0, g)),
                  kspec, vspec(LANE), kspec, vspec(2 * LANE)],
        out_specs=pl.BlockSpec((tq, gw), lambda g, i: (i, g)),
        out_shape=jax.ShapeDtypeStruct((s, NSA_HEADS * LANE), bf16),
        compiler_params=_params("parallel", "parallel"),
        name="selwin_attn",
    )(pa, oc, sel, pb, gh_pad, kst, vs, kwt, vw)


def _mlstm_kernel(qk_ref, v_ref, og_ref, if_ref, cw_ref, bias_ref, gh_ref, out_ref, buf, cst, mst, *, tm):
    @pl.when(pl.program_id(0) == 0)
    def _():
        buf[0:SUBLANE, :] = jnp.zeros((SUBLANE, 2 * M_WIDTH), f32)
        cst[...] = jnp.zeros(cst.shape, f32)
        mst[...] = jnp.zeros(mst.shape, f32)

    qk = qk_ref[...]
    buf[SUBLANE:SUBLANE + tm, :] = qk
    conv = cw_ref[M_CONV - 1:M_CONV, :] * qk
    for k in range(M_CONV - 1):
        off = SUBLANE - (M_CONV - 1) + k
        conv = conv + cw_ref[k:k + 1, :] * buf[off:off + tm, :]
    buf[0:SUBLANE, :] = qk[tm - SUBLANE:tm]
    qkc = conv * jax.nn.sigmoid(conv)
    qs = (qkc[:, :M_WIDTH] * (M_DH ** -0.5)).astype(bf16)
    ks = qkc[:, M_WIDTH:]
    vb = v_ref[...].astype(bf16)
    ogate = jax.nn.sigmoid(og_ref[...])

    pre = if_ref[...] + bias_ref[...]
    lane = lax.broadcasted_iota(jnp.int32, (M_CHUNK, LANE), 1)
    rowi = lax.broadcasted_iota(jnp.int32, (M_CHUNK, LANE), 0)
    logf = jnp.minimum(pre, 0.0) - jnp.log1p(jnp.exp(-jnp.abs(pre)))
    tri = (lax.broadcasted_iota(jnp.int32, (M_CHUNK, M_CHUNK), 0)
           >= lax.broadcasted_iota(jnp.int32, (M_CHUNK, M_CHUNK), 1))
    ones_col = (lax.broadcasted_iota(jnp.int32, (M_CHUNK, LANE), 1) == 0).astype(bf16)

    tn = (((1,), (1,)), ((), ()))
    tt = (((0,), (0,)), ((), ()))

    def state_part(c):
        r0, r1 = c * M_CHUNK, (c + 1) * M_CHUNK
        cum = logf[r0:r1]
        sft = 1
        while sft < M_CHUNK:
            cum = cum + jnp.where(rowi >= sft, pltpu.roll(cum, sft, 0), 0.0)
            sft *= 2
        comb = jnp.where(lane < M_HEADS, pre[r0:r1], cum)
        comb_t = comb.T
        part = []
        for h in range(M_HEADS):
            hs = slice(h * M_DH, (h + 1) * M_DH)
            bcol = comb[:, M_HEADS + h:M_HEADS + h + 1]
            icol = comb[:, h:h + 1]
            mprev = mst[0:1, h:h + 1]
            qh = qs[r0:r1, hs]
            kh = ks[r0:r1, hs]
            v_aug = jnp.concatenate([vb[r0:r1, hs], ones_col], axis=1)
            qk_s = lax.dot_general(qh, kh.astype(bf16), tn, preferred_element_type=f32)
            inter = _dot(qh, cst[h].astype(bf16))
            b_last = bcol[M_CHUNK - 1:M_CHUNK, :]
            g_s = b_last - bcol + icol
            m_new = jnp.maximum(b_last + mprev, jnp.max(g_s, axis=0, keepdims=True))
            w_s = jnp.exp(g_s - m_new)
            decay = jnp.exp(b_last + mprev - m_new)
            upd = lax.dot_general((kh * w_s).astype(bf16), v_aug, tt, preferred_element_type=f32)
            cst[h] = decay * cst[h] + upd
            mst[0:1, h:h + 1] = m_new
            part.append((bcol, mprev, qk_s, inter, v_aug))
        return comb_t, part

    def output_part(c, comb_t, part):
        r0, r1 = c * M_CHUNK, (c + 1) * M_CHUNK
        for h in range(M_HEADS):
            hs = slice(h * M_DH, (h + 1) * M_DH)
            bcol, mprev, qk_s, inter, v_aug = part[h]
            brow = comb_t[M_HEADS + h:M_HEADS + h + 1, :]
            irow = comb_t[h:h + 1, :]
            dmat = jnp.where(tri, bcol - brow + irow, -jnp.inf)
            m_inter = bcol + mprev
            m_t = jnp.maximum(jnp.max(dmat, axis=-1, keepdims=True), m_inter)
            sc = qk_s * jnp.exp(dmat - m_t)
            tot = jnp.exp(m_inter - m_t) * inter + _dot(sc.astype(bf16), v_aug)
            den = tot[:, M_DH:M_DH + 1]
            hout = tot[:, :M_DH] / jnp.maximum(jnp.abs(den), jnp.exp(-m_t))
            hn = hout * lax.rsqrt(jnp.mean(hout * hout, axis=-1, keepdims=True) + EPS) * gh_ref[:, hs]
            out_ref[r0:r1, hs] = (ogate[r0:r1, hs] * hn).astype(bf16)

    nc = tm // M_CHUNK
    cur = state_part(0)
    for c in range(nc):
        nxt = state_part(c + 1) if c + 1 < nc else None
        output_part(c, *cur)
        cur = nxt


def _mlstm(pb, conv_w, bias_if, g_head, tm=512):
    s = pb.shape[0]
    return pl.pallas_call(
        functools.partial(_mlstm_kernel, tm=tm),
        grid=(s // tm,),
        in_specs=[pl.BlockSpec((tm, 2 * M_WIDTH), lambda i: (i, B_MQ // (2 * M_WIDTH))),
                  pl.BlockSpec((tm, M_WIDTH), lambda i: (i, B_MV // M_WIDTH)),
                  pl.BlockSpec((tm, M_WIDTH), lambda i: (i, B_MO // M_WIDTH)),
                  pl.BlockSpec((tm, LANE), lambda i: (i, B_IF // LANE)),
                  _const_spec((M_CONV, 2 * M_WIDTH)),
                  _const_spec((1, LANE)),
                  _const_spec((1, M_WIDTH))],
        out_specs=pl.BlockSpec((tm, M_WIDTH), lambda i: (i, 0)),
        out_shape=jax.ShapeDtypeStruct((s, M_WIDTH), bf16),
        scratch_shapes=[pltpu.VMEM((tm + SUBLANE, 2 * M_WIDTH), f32),
                        pltpu.VMEM((M_HEADS, M_DH, 2 * M_DH), f32),
                        pltpu.VMEM((SUBLANE, LANE), f32)],
        compiler_params=_params("arbitrary"),
        name="mlstm",
    )(pb, pb, pb, pb, conv_w, bias_if, g_head)


def _memkv_kernel(mem_ref, g_ref, wk_ref, wv_ref, k_ref, v_ref):
    hm = _rms(mem_ref[...], g_ref[...]).astype(bf16)
    k_ref[...] = _dot(hm, wk_ref[...]).astype(bf16)
    v_ref[...] = _dot(hm, wv_ref[...]).astype(bf16)


def _memkv(mem, g, wk, wv):
    m = mem.shape[0]
    return pl.pallas_call(
        _memkv_kernel,
        out_shape=[jax.ShapeDtypeStruct((m, D_MODEL), bf16)] * 2,
        compiler_params=pltpu.CompilerParams(vmem_limit_bytes=VMEM_LIMIT),
        name="mem_kv",
    )(mem, g, wk, wv)


def _mix_mem_kernel(x_ref, nsa_ref, ml_ref, won_ref, wom_ref, gq_ref, wq_ref, kt_ref, v_ref, wo_ref, o_ref):
    x1 = x_ref[...] + _dot(nsa_ref[...], won_ref[...]) + _dot(ml_ref[...], wom_ref[...])
    hq = _rms(x1, gq_ref[...]).astype(bf16)
    q = (_dot(hq, wq_ref[...]) * (MEM_DH ** -0.5)).astype(bf16)
    outs = []
    heads = [slice(h * MEM_DH, (h + 1) * MEM_DH) for h in range(MEM_HEADS)]
    scores = [_dot(q[:, hs], kt_ref[hs, :]) for hs in heads]
    for hs, s in zip(heads, scores):
        e = jnp.exp(s - jnp.max(s, axis=-1, keepdims=True))
        p = e / jnp.sum(e, axis=-1, keepdims=True)
        outs.append(_dot(p.astype(bf16), v_ref[:, hs]).astype(bf16))
    o_ref[...] = x1 + _dot(jnp.concatenate(outs, axis=1), wo_ref[...])


def _mix_mem(x, nsa_n, ml, w_out_nsa, w_out_m, g_q, w_q, k_t, v, w_o, tm=512):
    s = x.shape[0]
    m = v.shape[0]
    return pl.pallas_call(
        _mix_mem_kernel,
        grid=(s // tm,),
        in_specs=[pl.BlockSpec((tm, D_MODEL), lambda i: (i, 0)),
                  pl.BlockSpec((tm, NSA_HEADS * LANE), lambda i: (i, 0)),
                  pl.BlockSpec((tm, M_WIDTH), lambda i: (i, 0)),
                  _const_spec((NSA_HEADS * LANE, D_MODEL)),
                  _const_spec((M_WIDTH, D_MODEL)),
                  _const_spec((1, D_MODEL)),
                  _const_spec((D_MODEL, D_MODEL)),
                  _const_spec((D_MODEL, m)),
                  _const_spec((m, D_MODEL)),
                  _const_spec((D_MODEL, D_MODEL))],
        out_specs=pl.BlockSpec((tm, D_MODEL), lambda i: (i, 0)),
        out_shape=jax.ShapeDtypeStruct((s, D_MODEL), f32),
        compiler_params=_params("parallel"),
        name="mix_mem",
    )(x, nsa_n, ml, w_out_nsa, w_out_m, g_q, w_q, k_t, v, w_o)


def _ffn_kernel(x_ref, xh_ref, g_ref, wup_ref, cw_ref, wdn_ref, gf_ref, o_ref, *, tm, widths):
    x = x_ref[...]
    hf = _rms(x, g_ref[...]).astype(bf16)
    hist = jnp.where(pl.program_id(0) > 0, 1.0, 0.0)
    hh = (_rms(xh_ref[...], g_ref[...]) * hist).astype(bf16)

    def conv_cols(c0, cw):
        w = wup_ref[:, c0:c0 + cw]
        u = _dot(hf, w)
        uh = _dot(hh, w)
        row = lax.broadcasted_iota(jnp.int32, (tm, cw), 0)
        y = cw_ref[F_CONV - 1:F_CONV, c0:c0 + cw] * u
        for k in range(F_CONV - 1):
            d = F_CONV - 1 - k
            sh = pltpu.roll(u, d, 0)
            for r in range(d):
                sh = jnp.where(row == r, uh[SUBLANE - d + r:SUBLANE - d + r + 1, :], sh)
            y = y + cw_ref[k:k + 1, c0:c0 + cw] * sh
        return y

    starts = [sum(widths[:n]) for n in range(len(widths))]
    both = lambda n: (conv_cols(starts[n], widths[n]), conv_cols(D_FF + starts[n], widths[n]))
    acc = x
    nxt = both(0)
    for n, (c0, cw) in enumerate(zip(starts, widths)):
        ya, yb = nxt
        if n + 1 < len(widths):
            nxt = both(n + 1)
        gated = _gelu(ya) * yb
        acc = acc + _dot(gated.astype(bf16), wdn_ref[c0:c0 + cw, :])
    o_ref[...] = _rms(acc, gf_ref[...])


def _ffn(x, g, w_up, conv_w, w_down, g_final, tm=1024, widths=(1280, 1536)):
    s = x.shape[0]
    halo = tm // SUBLANE
    assert sum(widths) == D_FF
    return pl.pallas_call(
        functools.partial(_ffn_kernel, tm=tm, widths=widths),
        grid=(s // tm,),
        in_specs=[pl.BlockSpec((tm, D_MODEL), lambda i: (i, 0)),
                  pl.BlockSpec((SUBLANE, D_MODEL), lambda i: (jnp.maximum(i * halo - 1, 0), 0)),
                  _const_spec((1, D_MODEL)),
                  _const_spec((D_MODEL, 2 * D_FF)),
                  _const_spec((F_CONV, 2 * D_FF)),
                  _const_spec((D_FF, D_MODEL)),
                  _const_spec((1, D_MODEL))],
        out_specs=pl.BlockSpec((tm, D_MODEL), lambda i: (i, 0)),
        out_shape=jax.ShapeDtypeStruct((s, D_MODEL), f32),
        compiler_params=_params("parallel"),
        name="ffn",
    )(x, x, g, w_up, conv_w, w_down, g_final)


def _pad_cols(a, n):
    return jnp.pad(a, ((0, 0), (0, n - a.shape[1])))


def _pad_heads(a, heads, dh):
    r = a.shape[0]
    return jnp.pad(a.reshape(r, heads, dh), ((0, 0), (0, 0), (0, LANE - dh))).reshape(r, heads * LANE)


def _build_w_in(w):
    o = 0
    parts = {}
    sizes = (("q", 512), ("kc", 128), ("vc", 128), ("ks", 128), ("vs", 128), ("kw", 128), ("vw", 128),
             ("gt", 24), ("mq", 512), ("mk", 512), ("mv", 512), ("mo", 512), ("mi", 4), ("mf", 4))
    for name, n in sizes:
        parts[name] = w[:, o:o + n]
        o += n
    gt = parts["gt"].reshape(D_MODEL, NSA_GROUPS, NSA_HPG * 3)
    cols = [_pad_heads(parts["q"], NSA_HEADS, NSA_DH), parts["ks"], parts["vs"], parts["kw"], parts["vw"],
            parts["mq"], parts["mk"], parts["mv"], parts["mo"], parts["kc"], parts["vc"],
            _pad_cols(gt[:, 0], LANE), _pad_cols(gt[:, 1], LANE),
            _pad_cols(jnp.concatenate([parts["mi"], parts["mf"]], axis=1), LANE)]
    return jnp.concatenate(cols, axis=1).astype(bf16)


def _block_diag2(w):
    z = jnp.zeros_like(w)
    return jnp.concatenate([jnp.concatenate([w, z], axis=-1), jnp.concatenate([z, w], axis=-1)], axis=-2)


def kernel(x, mem, g_mix, w_in, cmp_pos_k, cmp_w1_k, cmp_w2_k, cmp_pos_v, cmp_w1_v, cmp_w2_v, mlstm_conv_w, mlstm_b_i, mlstm_b_f, g_head_nsa, g_head_mlstm, w_out, g_mem_q, g_mem_kv, w_mem_q, w_mem_k, w_mem_v, w_mem_o, g_ffn, w_up, ffn_conv_w, w_down, g_final):
    b, s, _ = x.shape
    assert b == 1 and g_mix.shape[0] == 1
    sel_tk = 2048
    assert s % 8192 == 0 and s >= sel_tk + WINDOW
    l = 0
    xs = x[0]
    row = lambda a: a.reshape(1, -1)

    qa, kst, vs, kwt, vw, pb = _inproj(xs, row(g_mix[l]), _build_w_in(w_in[l]), sel_tk // SEL_BLOCK)

    n16 = s // CMP_STRIDE
    dup = lambda p: jnp.concatenate([p, p], axis=-1)
    pos = jnp.stack([dup(cmp_pos_k[l]), dup(cmp_pos_v[l])])
    w1 = _block_diag2(jnp.stack([cmp_w1_k[l], cmp_w1_v[l]]).reshape(2, CMP_LEN, NSA_DH, -1)).astype(bf16)
    w2 = _block_diag2(jnp.stack([cmp_w2_k[l], cmp_w2_v[l]])).astype(bf16)
    kct, vc = _compress(pb, pos, w1, w2)

    n_sel = s // SEL_BLOCK
    cstart = jnp.arange(n16)[:, None] * CMP_STRIDE
    sstart = jnp.arange(n_sel)[None, :] * SEL_BLOCK
    overlap = ((cstart < sstart + SEL_BLOCK) & (cstart + CMP_LEN > sstart)).astype(bf16)
    oc, sel = _cmp_attention(qa, kct, vc, overlap)

    gh_pad = _pad_heads(row(g_head_nsa[l]), NSA_HEADS, NSA_DH)
    nsa_n = _selwin_attention(qa, pb, oc, sel, gh_pad, kst, vs, kwt, vw, tk=sel_tk)

    bias_if = _pad_cols(jnp.concatenate([row(mlstm_b_i[l]), row(mlstm_b_f[l])], axis=1), LANE)
    ml = _mlstm(pb, mlstm_conv_w[l], bias_if, row(g_head_mlstm[l]))

    w_out_nsa = jnp.pad(w_out[l][:NSA_HEADS * NSA_DH].reshape(NSA_HEADS, NSA_DH, D_MODEL),
                        ((0, 0), (0, LANE - NSA_DH), (0, 0))).reshape(NSA_HEADS * LANE, D_MODEL).astype(bf16)
    w_out_m = w_out[l][NSA_HEADS * NSA_DH:].astype(bf16)
    k_mem, v_mem = _memkv(mem[0], row(g_mem_kv[l]), w_mem_k[l].astype(bf16), w_mem_v[l].astype(bf16))
    x2 = _mix_mem(xs, nsa_n, ml, w_out_nsa, w_out_m, row(g_mem_q[l]), w_mem_q[l].astype(bf16),
                  k_mem.T, v_mem, w_mem_o[l].astype(bf16))

    out = _ffn(x2, row(g_ffn[l]), w_up[l].astype(bf16), ffn_conv_w[l], w_down[l].astype(bf16), row(g_final))
    return out[None]
```

```python
import functools

import jax
import jax.numpy as jnp
from jax import lax
from jax.experimental import pallas as pl
from jax.experimental.pallas import tpu as pltpu

f32 = jnp.float32
bf16 = jnp.bfloat16

D_MODEL = 1024
NSA_HEADS = 8
NSA_GROUPS = 2
NSA_HPG = NSA_HEADS // NSA_GROUPS
NSA_DH = 64
CMP_LEN = 32
CMP_STRIDE = 16
SEL_BLOCK = 64
SEL_TOPK = 16
WINDOW = 512
FORCE_BONUS = 1.0e4
M_HEADS = 4
M_DH = 128
M_WIDTH = M_HEADS * M_DH
M_CONV = 4
M_CHUNK = 64
MEM_HEADS = 4
MEM_DH = D_MODEL // MEM_HEADS
D_FF = 2816
F_CONV = 3
EPS = 1e-6
NSA_QSCALE = NSA_DH ** -0.5 * 1.4426950408889634

LANE = 128
SUBLANE = 8
NEG = -1e30
VMEM_LIMIT = 56 * 1024 * 1024

A_Q, A_KS, A_VS, A_KW, A_VW, NA = 0, 1024, 1152, 1280, 1408, 1536
B_MQ, B_MV, B_MO, B_KC, B_VC, B_G0, B_IF, NB = 0, 1024, 1536, 2048, 2176, 2304, 2560, 2688


def _dot(a, b):
    return jnp.dot(a, b, preferred_element_type=f32)


def _rms(x, g):
    return x * lax.rsqrt(jnp.mean(x * x, axis=-1, keepdims=True) + EPS) * g


def _gelu(x):
    return 0.5 * x * (1.0 + lax.erf(x * 0.7071067811865476))


def _params(*sem):
    return pltpu.CompilerParams(dimension_semantics=sem, vmem_limit_bytes=VMEM_LIMIT)


def _const_spec(shape):
    nd = len(shape)
    return pl.BlockSpec(shape, lambda *_: (0,) * nd, pipeline_mode=pl.Buffered(1))


def _split_keys_t(ref, k, extra=None):
    kt = k.T
    n_extra = 0 if extra is None else extra.shape[0]
    for g in range(NSA_GROUPS):
        ref[g, 0:NSA_DH, :] = kt[g * NSA_DH:(g + 1) * NSA_DH].astype(bf16)
        if extra is not None:
            ref[g, NSA_DH:NSA_DH + n_extra, :] = extra
        ref[g, NSA_DH + n_extra:, :] = jnp.zeros((LANE - NSA_DH - n_extra, kt.shape[1]), bf16)


def _split_values(ref, v, ones):
    lane = lax.broadcasted_iota(jnp.int32, v.shape, 1)
    fill = jnp.where(lane == NSA_DH, 1.0, 0.0) if ones == "col" else 0.0
    for g in range(NSA_GROUPS):
        vg = v if g == 0 else pltpu.roll(v, LANE - g * NSA_DH, 1)
        ref[g, :, 0:LANE] = jnp.where(lane < NSA_DH, vg, fill).astype(bf16)
        if ones == "tile":
            ref[g, :, LANE:2 * LANE] = jnp.ones(v.shape, bf16)


def _inproj_kernel(x_ref, g_ref, w_ref, q_ref, kst_ref, vs_ref, kwt_ref, vw_ref, pb_ref, *, tm, nblk):
    h = _rms(x_ref[...], g_ref[...]).astype(bf16)
    q_ref[...] = (_dot(h, w_ref[:, A_Q:A_KS]) * NSA_QSCALE).astype(bf16)
    kv = _dot(h, w_ref[:, A_KS:NA])
    key = pl.program_id(0) * tm + lax.broadcasted_iota(jnp.int32, (nblk, tm), 1)
    rowj = lax.broadcasted_iota(jnp.int32, (nblk, tm), 0)
    ind = ((lax.shift_right_logical(key, 6) & (nblk - 1)) == rowj).astype(bf16)
    _split_keys_t(kst_ref, kv[:, 0:LANE], ind)
    _split_values(vs_ref, kv[:, LANE:2 * LANE], "col")
    _split_keys_t(kwt_ref, kv[:, 2 * LANE:3 * LANE])
    _split_values(vw_ref, kv[:, 3 * LANE:4 * LANE], "tile")
    for c0 in range(0, NB, 512):
        c1 = min(c0 + 512, NB)
        pb_ref[:, c0:c1] = _dot(h, w_ref[:, NA + c0:NA + c1])


def _inproj(x, g, w, nblk, tm=512):
    s = x.shape[0]
    assert nblk & (nblk - 1) == 0 and SEL_BLOCK == 64
    kspec = pl.BlockSpec((NSA_GROUPS, LANE, tm), lambda i: (0, 0, i))
    vspec = lambda w: pl.BlockSpec((NSA_GROUPS, tm, w), lambda i: (0, i, 0))
    kshape = jax.ShapeDtypeStruct((NSA_GROUPS, LANE, s), bf16)
    vshape = lambda w: jax.ShapeDtypeStruct((NSA_GROUPS, s, w), bf16)
    return pl.pallas_call(
        functools.partial(_inproj_kernel, tm=tm, nblk=nblk),
        grid=(s // tm,),
        in_specs=[pl.BlockSpec((tm, D_MODEL), lambda i: (i, 0)),
                  _const_spec((1, D_MODEL)),
                  _const_spec((D_MODEL, NA + NB))],
        out_specs=[pl.BlockSpec((tm, NSA_HEADS * LANE), lambda i: (i, 0)), kspec, vspec(LANE), kspec, vspec(2 * LANE),
                   pl.BlockSpec((tm, NB), lambda i: (i, 0))],
        out_shape=[jax.ShapeDtypeStruct((s, NSA_HEADS * LANE), bf16), kshape, vshape(LANE), kshape, vshape(2 * LANE),
                   jax.ShapeDtypeStruct((s, NB), f32)],
        compiler_params=_params("parallel"),
        name="inproj",
    )(x, g, w)


def _compress_kernel(kc_ref, vc_ref, pos_ref, w1_ref, w2_ref, kct_ref, vcp_ref):
    n = kc_ref.shape[0] // CMP_STRIDE

    def mlp(x_ref, a):
        top = bot = None
        for l in range(CMP_STRIDE):
            xl = x_ref[pl.ds(l, n, stride=CMP_STRIDE), :]
            t = _dot((xl + pos_ref[a, l:l + 1, :]).astype(bf16), w1_ref[a, l])
            b = _dot((xl + pos_ref[a, CMP_STRIDE + l:CMP_STRIDE + l + 1, :]).astype(bf16), w1_ref[a, CMP_STRIDE + l])
            top = t if top is None else top + t
            bot = b if bot is None else bot + b
        pre = top + pltpu.roll(bot, n - 1, 0)
        return _dot(_gelu(pre).astype(bf16), w2_ref[a])

    _split_keys_t(kct_ref, mlp(kc_ref, 0))
    _split_values(vcp_ref, mlp(vc_ref, 1), None)


def _compress(pb, pos, w1, w2):
    s = pb.shape[0]
    n = s // CMP_STRIDE
    return pl.pallas_call(
        _compress_kernel,
        grid=(1,),
        in_specs=[pl.BlockSpec((s, LANE), lambda i: (0, B_KC // LANE)),
                  pl.BlockSpec((s, LANE), lambda i: (0, B_VC // LANE)),
                  _const_spec(pos.shape), _const_spec(w1.shape), _const_spec(w2.shape)],
        out_specs=[pl.BlockSpec((NSA_GROUPS, LANE, n), lambda i: (0, 0, 0)),
                   pl.BlockSpec((NSA_GROUPS, n, LANE), lambda i: (0, 0, 0))],
        out_shape=[jax.ShapeDtypeStruct((NSA_GROUPS, LANE, n), bf16),
                   jax.ShapeDtypeStruct((NSA_GROUPS, n, LANE), bf16)],
        compiler_params=_params("arbitrary"),
        name="compress",
    )(pb, pb, pos, w1, w2)


def _cmp_kernel(q_ref, kct_ref, vc_ref, ov_ref, oc_ref, sel_ref, *, tq, steps, classes):
    nsel = ov_ref.shape[1]
    t0 = pl.program_id(0) * tq
    tpos = t0 + lax.broadcasted_iota(jnp.int32, (tq, 1), 0)
    any_vis = tpos >= CMP_LEN - 1
    cur = lax.shift_right_logical(tpos, 6)

    def body(ncmp, nsel_w):
        cend = lax.broadcasted_iota(jnp.int32, (1, ncmp), 1) * CMP_STRIDE + (CMP_LEN - 1)
        vis = cend <= tpos
        blk = lax.broadcasted_iota(jnp.int32, (1, nsel_w), 1)
        forced = (blk == 0) | (blk == cur) | (blk == cur - 1)
        scores = []
        qk = lambda h: _dot(q_ref[:, h * LANE:(h + 1) * LANE], kct_ref[h // NSA_HPG, :, 0:ncmp])
        ahead = 2
        pending = [qk(h) for h in range(ahead)]
        for g in range(NSA_GROUPS):
            psum = None
            for hh in range(NSA_HPG):
                h = g * NSA_HPG + hh
                c0 = h * LANE
                s = jnp.where(vis, pending.pop(0), NEG)
                if h + ahead < NSA_HEADS:
                    pending.append(qk(h + ahead))
                e = jnp.exp2(s - jnp.max(s, axis=-1, keepdims=True))
                l = jnp.sum(e, axis=-1, keepdims=True)
                p = e * jnp.where(any_vis, 1.0 / l, 0.0)
                oc_ref[:, c0:c0 + LANE] = _dot(p.astype(bf16), vc_ref[g, 0:ncmp, :])
                psum = p if psum is None else psum + p
            imp = _dot(psum.astype(bf16), ov_ref[0:ncmp, 0:nsel_w])
            scores.append(jnp.where(forced, -jnp.inf, jnp.where(blk <= cur, imp, -FORCE_BONUS)))
        score = jnp.concatenate(scores, axis=0).T
        blkt = lax.broadcasted_iota(jnp.int32, score.shape, 0).astype(f32)
        for _ in range(SEL_TOPK - 3):
            mx = jnp.max(score, axis=0, keepdims=True)
            j = jnp.min(jnp.where(score == mx, blkt, float(nsel_w)), axis=0, keepdims=True)
            score = jnp.where(blkt == j, -jnp.inf, score)
        selm = (score == -jnp.inf).astype(f32).T
        for g in range(NSA_GROUPS):
            sel_ref[g, :, 0:nsel_w] = selm[g * tq:(g + 1) * tq]
            if nsel_w < nsel:
                sel_ref[g, :, nsel_w:] = jnp.zeros((tq, nsel - nsel_w), f32)

    per = steps // classes
    for k in range(classes):
        @pl.when((pl.program_id(0) >= k * per) & (pl.program_id(0) < (k + 1) * per))
        def _(k=k):
            body(kct_ref.shape[2] * (k + 1) // classes, nsel * (k + 1) // classes)


def _cmp_attention(pa, kct, vc, ov, tq=512, classes=4):
    s = pa.shape[0]
    ncmp, nsel = ov.shape
    steps = s // tq
    assert steps % classes == 0 and (ncmp // classes) % LANE == 0 and (nsel // classes) % SUBLANE == 0
    return pl.pallas_call(
        functools.partial(_cmp_kernel, tq=tq, steps=steps, classes=classes),
        grid=(steps,),
        in_specs=[pl.BlockSpec((tq, NSA_HEADS * LANE), lambda i: (i, 0)),
                  _const_spec((NSA_GROUPS, LANE, ncmp)),
                  _const_spec((NSA_GROUPS, ncmp, LANE)),
                  _const_spec((ncmp, nsel))],
        out_specs=[pl.BlockSpec((tq, NSA_HEADS * LANE), lambda i: (i, 0)),
                   pl.BlockSpec((NSA_GROUPS, tq, nsel), lambda i: (0, i, 0))],
        out_shape=[jax.ShapeDtypeStruct((s, NSA_HEADS * LANE), f32),
                   jax.ShapeDtypeStruct((NSA_GROUPS, s, nsel), f32)],
        compiler_params=_params("parallel"),
        name="cmp_attn",
    )(pa, kct, vc, ov)


def _selwin_kernel(q_ref, oc_ref, sel_ref, gate_ref, gh_ref, kst_ref, vs_ref, kwt_ref, vw_ref, out_ref, *, tq, tk):
    nsel = sel_ref.shape[2]
    nblk = tk // SEL_BLOCK
    t0 = pl.program_id(1) * tq
    tpos = t0 + lax.broadcasted_iota(jnp.int32, (tq, 1), 0)
    q4 = [q_ref[:, hh * LANE:(hh + 1) * LANE] for hh in range(NSA_HPG)]
    sel = sel_ref[0]
    lane = lax.broadcasted_iota(jnp.int32, (tq, LANE), 1)
    bias_lanes = (lane >= NSA_DH) & (lane < NSA_DH + nblk)

    def q_biased(kt):
        sh = lax.rem(NSA_DH + nsel - kt * nblk, jnp.int32(nsel))
        bias = ((pltpu.roll(sel, sh, 1)[:, :LANE] - 1.0) * -NEG).astype(bf16)
        return [jnp.where(bias_lanes, bias, qh) for qh in q4]

    def attend(qb, st, width, carry, causal):
        kblk = kst_ref[0, :, pl.ds(st, width)]
        vblk = vs_ref[0, pl.ds(st, width), :]
        if causal:
            keep = st + lax.broadcasted_iota(jnp.int32, (1, width), 1) <= tpos
        out = []
        s_next = _dot(qb[0], kblk)
        for hh in range(NSA_HPG):
            m, acc = carry[hh]
            s = s_next
            if hh + 1 < NSA_HPG:
                s_next = _dot(qb[hh + 1], kblk)
            if causal:
                s = jnp.where(keep, s, NEG)
            mn = jnp.maximum(m, jnp.max(s, axis=-1, keepdims=True))
            p = jnp.exp2(s - mn)
            out.append((mn, jnp.exp2(m - mn) * acc + _dot(p.astype(bf16), vblk)))
        return tuple(out)

    last = t0 // tk
    init = tuple((jnp.full((tq, 1), NEG, f32), jnp.zeros((tq, LANE), f32)) for _ in range(NSA_HPG))
    carry = lax.fori_loop(
        0, last, lambda kt, c: attend(q_biased(kt), pl.multiple_of(kt * tk, tk), tk, c, False), init)
    qb_last = q_biased(last)
    base = last * tk
    carry = lax.fori_loop(
        0, (t0 - base) // tq,
        lambda j, c: attend(qb_last, pl.multiple_of(base + j * tq, tq), tq, c, False), carry)

    parts = 4
    hq = tq // parts
    rows = lambda r: slice(r * hq, (r + 1) * hq)
    dparts = 2
    dq = tq // dparts
    drows = lambda r: slice(r * dq, (r + 1) * dq)
    diag = []
    for r in range(dparts):
        tp = t0 + r * dq + lax.broadcasted_iota(jnp.int32, (dq, 1), 0)
        width = (r + 1) * dq
        kpos = t0 + lax.broadcasted_iota(jnp.int32, (1, width), 1)
        diag.append((kst_ref[0, :, pl.ds(pl.multiple_of(t0, tq), width)],
                     vs_ref[0, pl.ds(pl.multiple_of(t0, tq), width), :], kpos <= tp))
    win = []
    for r in range(parts):
        tp = t0 + r * hq + lax.broadcasted_iota(jnp.int32, (hq, 1), 0)
        st = pl.multiple_of(jnp.maximum(t0 + r * hq - WINDOW, 0), LANE)
        kpos = st + lax.broadcasted_iota(jnp.int32, (1, hq + WINDOW), 1)
        win.append((kwt_ref[0, :, pl.ds(st, hq + WINDOW)], vw_ref[0, pl.ds(st, hq + WINDOW), :],
                    (kpos <= tp) & (kpos > tp - WINDOW)))
    chains = [c for hh in range(NSA_HPG)
              for c in [("diag", hh, r) for r in range(dparts)] + [("win", hh, r) for r in range(parts)]]
    qk = lambda br, hh, r: (_dot(qb_last[hh][drows(r)], diag[r][0]) if br == "diag"
                            else _dot(q4[hh][rows(r)], win[r][0]))

    ngate = 3 * NSA_HPG
    erow = lax.broadcasted_iota(jnp.int32, (LANE, ngate * LANE), 0)
    ecol = lax.broadcasted_iota(jnp.int32, (LANE, ngate * LANE), 1)
    expand = (erow == lax.shift_right_logical(ecol, 7)).astype(bf16)
    rest = jax.nn.sigmoid(gate_ref[...])
    gates = None
    for _ in range(3):
        term = rest.astype(bf16)
        rest = rest - term.astype(f32)
        part = _dot(term, expand)
        gates = part if gates is None else gates + part
    gate = lambda hh, j, r: gates[rows(r), (3 * hh + j) * LANE:(3 * hh + j + 1) * LANE]

    valid = lax.broadcasted_iota(jnp.int32, (hq, LANE), 1) < NSA_DH
    ahead = 2
    pending = [qk(*c) for c in chains[:ahead]]
    o_parts = []
    for n, (br, hh, r) in enumerate(chains):
        s = pending.pop(0)
        if n + ahead < len(chains):
            pending.append(qk(*chains[n + ahead]))
        if br == "diag":
            _, vdiag, dmask = diag[r]
            m, acc = carry[hh][0][drows(r)], carry[hh][1][drows(r)]
            s = jnp.where(dmask, s, NEG)
            mn = jnp.maximum(m, jnp.max(s, axis=-1, keepdims=True))
            acc_s = jnp.exp2(m - mn) * acc + _dot(jnp.exp2(s - mn).astype(bf16), vdiag)
            o_parts.append(acc_s / acc_s[:, NSA_DH:NSA_DH + 1])
            if r + 1 == dparts:
                o_sel = jnp.concatenate(o_parts, axis=0)
                o_parts = []
            continue
        _, vwin, wmask = win[r]
        s = jnp.where(wmask, s, NEG)
        pw = jnp.exp2(s - jnp.max(s, axis=-1, keepdims=True))
        acc_w = _dot(pw.astype(bf16), vwin)
        o_w = acc_w[:, :LANE] * (gate(hh, 2, r) / acc_w[:, LANE:])
        o = gate(hh, 0, r) * oc_ref[rows(r), hh * LANE:(hh + 1) * LANE] + gate(hh, 1, r) * o_sel[rows(r)] + o_w
        o = jnp.where(valid, o, 0.0)
        ms = jnp.sum(o * o, axis=-1, keepdims=True) * (1.0 / NSA_DH)
        y = o * lax.rsqrt(ms + EPS) * gh_ref[:, hh * LANE:(hh + 1) * LANE]
        out_ref[rows(r), hh * LANE:(hh + 1) * LANE] = y.astype(bf16)


def _selwin_attention(pa, pb, oc, sel, gh_pad, kst, vs, kwt, vw, tq=512, tk=2048):
    s = pa.shape[0]
    nsel = sel.shape[2]
    gw = NSA_HPG * LANE
    assert tk % tq == 0 and s % tk == 0
    kspec = pl.BlockSpec((1, LANE, s), lambda g, i: (g, 0, 0), pipeline_mode=pl.Buffered(1))
    vspec = lambda w: pl.BlockSpec((1, s, w), lambda g, i: (g, 0, 0), pipeline_mode=pl.Buffered(1))
    return pl.pallas_call(
        functools.partial(_selwin_kernel, tq=tq, tk=tk),
        grid=(NSA_GROUPS, s // tq),
        in_specs=[pl.BlockSpec((tq, gw), lambda g, i: (i, g)),
                  pl.BlockSpec((tq, gw), lambda g, i: (i, g)),
                  pl.BlockSpec((1, tq, nsel), lambda g, i: (g, i, 0)),
                  pl.BlockSpec((tq, LANE), lambda g, i: (i, B_G0 // LANE + g)),
                  pl.BlockSpec((1, gw), lambda g, i: (0, g)),
                  kspec, vspec(LANE), kspec, vspec(2 * LANE)],
        out_specs=pl.BlockSpec((tq, gw), lambda g, i: (i, g)),
        out_shape=jax.ShapeDtypeStruct((s, NSA_HEADS * LANE), bf16),
        compiler_params=_params("parallel", "parallel"),
        name="selwin_attn",
    )(pa, oc, sel, pb, gh_pad, kst, vs, kwt, vw)


def _mlstm_kernel(qk_ref, v_ref, og_ref, if_ref, cw_ref, bias_ref, gh_ref, out_ref, buf, cst, mst, *, tm):
    @pl.when(pl.program_id(0) == 0)
    def _():
        buf[0:SUBLANE, :] = jnp.zeros((SUBLANE, 2 * M_WIDTH), f32)
        cst[...] = jnp.zeros(cst.shape, f32)
        mst[...] = jnp.zeros(mst.shape, f32)

    qk = qk_ref[...]
    buf[SUBLANE:SUBLANE + tm, :] = qk
    conv = cw_ref[M_CONV - 1:M_CONV, :] * qk
    for k in range(M_CONV - 1):
        off = SUBLANE - (M_CONV - 1) + k
        conv = conv + cw_ref[k:k + 1, :] * buf[off:off + tm, :]
    buf[0:SUBLANE, :] = qk[tm - SUBLANE:tm]
    qkc = conv * jax.nn.sigmoid(conv)
    qs = (qkc[:, :M_WIDTH] * (M_DH ** -0.5)).astype(bf16)
    ks = qkc[:, M_WIDTH:]
    vb = v_ref[...].astype(bf16)
    ogate = jax.nn.sigmoid(og_ref[...])

    pre = if_ref[...] + bias_ref[...]
    lane = lax.broadcasted_iota(jnp.int32, (M_CHUNK, LANE), 1)
    rowi = lax.broadcasted_iota(jnp.int32, (M_CHUNK, LANE), 0)
    logf = jnp.minimum(pre, 0.0) - jnp.log1p(jnp.exp(-jnp.abs(pre)))
    tri = (lax.broadcasted_iota(jnp.int32, (M_CHUNK, M_CHUNK), 0)
           >= lax.broadcasted_iota(jnp.int32, (M_CHUNK, M_CHUNK), 1))
    ones_col = (lax.broadcasted_iota(jnp.int32, (M_CHUNK, LANE), 1) == 0).astype(bf16)

    tn = (((1,), (1,)), ((), ()))
    tt = (((0,), (0,)), ((), ()))

    def state_part(c):
        r0, r1 = c * M_CHUNK, (c + 1) * M_CHUNK
        cum = logf[r0:r1]
        sft = 1
        while sft < M_CHUNK:
            cum = cum + jnp.where(rowi >= sft, pltpu.roll(cum, sft, 0), 0.0)
            sft *= 2
        comb = jnp.where(lane < M_HEADS, pre[r0:r1], cum)
        comb_t = comb.T
        part = []
        for h in range(M_HEADS):
            hs = slice(h * M_DH, (h + 1) * M_DH)
            bcol = comb[:, M_HEADS + h:M_HEADS + h + 1]
            icol = comb[:, h:h + 1]
            mprev = mst[0:1, h:h + 1]
            qh = qs[r0:r1, hs]
            kh = ks[r0:r1, hs]
            v_aug = jnp.concatenate([vb[r0:r1, hs], ones_col], axis=1)
            qk_s = lax.dot_general(qh, kh.astype(bf16), tn, preferred_element_type=f32)
            inter = _dot(qh, cst[h].astype(bf16))
            b_last = bcol[M_CHUNK - 1:M_CHUNK, :]
            g_s = b_last - bcol + icol
            m_new = jnp.maximum(b_last + mprev, jnp.max(g_s, axis=0, keepdims=True))
            w_s = jnp.exp(g_s - m_new)
            decay = jnp.exp(b_last + mprev - m_new)
            upd = lax.dot_general((kh * w_s).astype(bf16), v_aug, tt, preferred_element_type=f32)
            cst[h] = decay * cst[h] + upd
            mst[0:1, h:h + 1] = m_new
            part.append((bcol, mprev, qk_s, inter, v_aug))
        return comb_t, part

    def output_part(c, comb_t, part):
        r0, r1 = c * M_CHUNK, (c + 1) * M_CHUNK
        for h in range(M_HEADS):
            hs = slice(h * M_DH, (h + 1) * M_DH)
            bcol, mprev, qk_s, inter, v_aug = part[h]
            brow = comb_t[M_HEADS + h:M_HEADS + h + 1, :]
            irow = comb_t[h:h + 1, :]
            dmat = jnp.where(tri, bcol - brow + irow, -jnp.inf)
            m_inter = bcol + mprev
            m_t = jnp.maximum(jnp.max(dmat, axis=-1, keepdims=True), m_inter)
            sc = qk_s * jnp.exp(dmat - m_t)
            tot = jnp.exp(m_inter - m_t) * inter + _dot(sc.astype(bf16), v_aug)
            den = tot[:, M_DH:M_DH + 1]
            hout = tot[:, :M_DH] / jnp.maximum(jnp.abs(den), jnp.exp(-m_t))
            hn = hout * lax.rsqrt(jnp.mean(hout * hout, axis=-1, keepdims=True) + EPS) * gh_ref[:, hs]
            out_ref[r0:r1, hs] = (ogate[r0:r1, hs] * hn).astype(bf16)

    nc = tm // M_CHUNK
    cur = state_part(0)
    for c in range(nc):
        nxt = state_part(c + 1) if c + 1 < nc else None
        output_part(c, *cur)
        cur = nxt


def _mlstm(pb, conv_w, bias_if, g_head, tm=512):
    s = pb.shape[0]
    return pl.pallas_call(
        functools.partial(_mlstm_kernel, tm=tm),
        grid=(s // tm,),
        in_specs=[pl.BlockSpec((tm, 2 * M_WIDTH), lambda i: (i, B_MQ // (2 * M_WIDTH))),
                  pl.BlockSpec((tm, M_WIDTH), lambda i: (i, B_MV // M_WIDTH)),
                  pl.BlockSpec((tm, M_WIDTH), lambda i: (i, B_MO // M_WIDTH)),
                  pl.BlockSpec((tm, LANE), lambda i: (i, B_IF // LANE)),
                  _const_spec((M_CONV, 2 * M_WIDTH)),
                  _const_spec((1, LANE)),
                  _const_spec((1, M_WIDTH))],
        out_specs=pl.BlockSpec((tm, M_WIDTH), lambda i: (i, 0)),
        out_shape=jax.ShapeDtypeStruct((s, M_WIDTH), bf16),
        scratch_shapes=[pltpu.VMEM((tm + SUBLANE, 2 * M_WIDTH), f32),
                        pltpu.VMEM((M_HEADS, M_DH, 2 * M_DH), f32),
                        pltpu.VMEM((SUBLANE, LANE), f32)],
        compiler_params=_params("arbitrary"),
        name="mlstm",
    )(pb, pb, pb, pb, conv_w, bias_if, g_head)


def _memkv_kernel(mem_ref, g_ref, wk_ref, wv_ref, k_ref, v_ref):
    hm = _rms(mem_ref[...], g_ref[...]).astype(bf16)
    k_ref[...] = _dot(hm, wk_ref[...]).astype(bf16)
    v_ref[...] = _dot(hm, wv_ref[...]).astype(bf16)


def _memkv(mem, g, wk, wv):
    m = mem.shape[0]
    return pl.pallas_call(
        _memkv_kernel,
        out_shape=[jax.ShapeDtypeStruct((m, D_MODEL), bf16)] * 2,
        compiler_params=pltpu.CompilerParams(vmem_limit_bytes=VMEM_LIMIT),
        name="mem_kv",
    )(mem, g, wk, wv)


def _mix_mem_kernel(x_ref, nsa_ref, ml_ref, won_ref, wom_ref, gq_ref, wq_ref, kt_ref, v_ref, wo_ref, o_ref):
    x1 = x_ref[...] + _dot(nsa_ref[...], won_ref[...]) + _dot(ml_ref[...], wom_ref[...])
    hq = _rms(x1, gq_ref[...]).astype(bf16)
    q = (_dot(hq, wq_ref[...]) * (MEM_DH ** -0.5)).astype(bf16)
    outs = []
    heads = [slice(h * MEM_DH, (h + 1) * MEM_DH) for h in range(MEM_HEADS)]
    scores = [_dot(q[:, hs], kt_ref[hs, :]) for hs in heads]
    for hs, s in zip(heads, scores):
        e = jnp.exp(s - jnp.max(s, axis=-1, keepdims=True))
        p = e / jnp.sum(e, axis=-1, keepdims=True)
        outs.append(_dot(p.astype(bf16), v_ref[:, hs]).astype(bf16))
    o_ref[...] = x1 + _dot(jnp.concatenate(outs, axis=1), wo_ref[...])


def _mix_mem(x, nsa_n, ml, w_out_nsa, w_out_m, g_q, w_q, k_t, v, w_o, tm=1024):
    s = x.shape[0]
    m = v.shape[0]
    return pl.pallas_call(
        _mix_mem_kernel,
        grid=(s // tm,),
        in_specs=[pl.BlockSpec((tm, D_MODEL), lambda i: (i, 0)),
                  pl.BlockSpec((tm, NSA_HEADS * LANE), lambda i: (i, 0)),
                  pl.BlockSpec((tm, M_WIDTH), lambda i: (i, 0)),
                  _const_spec((NSA_HEADS * LANE, D_MODEL)),
                  _const_spec((M_WIDTH, D_MODEL)),
                  _const_spec((1, D_MODEL)),
                  _const_spec((D_MODEL, D_MODEL)),
                  _const_spec((D_MODEL, m)),
                  _const_spec((m, D_MODEL)),
                  _const_spec((D_MODEL, D_MODEL))],
        out_specs=pl.BlockSpec((tm, D_MODEL), lambda i: (i, 0)),
        out_shape=jax.ShapeDtypeStruct((s, D_MODEL), f32),
        compiler_params=_params("parallel"),
        name="mix_mem",
    )(x, nsa_n, ml, w_out_nsa, w_out_m, g_q, w_q, k_t, v, w_o)


def _ffn_kernel(x_ref, xh_ref, g_ref, wup_ref, cw_ref, wdn_ref, gf_ref, o_ref, *, tm, widths):
    x = x_ref[...]
    hf = _rms(x, g_ref[...]).astype(bf16)
    hist = jnp.where(pl.program_id(0) > 0, 1.0, 0.0)
    hh = (_rms(xh_ref[...], g_ref[...]) * hist).astype(bf16)

    def conv_cols(c0, cw):
        w = wup_ref[:, c0:c0 + cw]
        u = _dot(hf, w)
        uh = _dot(hh, w)
        row = lax.broadcasted_iota(jnp.int32, (tm, cw), 0)
        y = cw_ref[F_CONV - 1:F_CONV, c0:c0 + cw] * u
        for k in range(F_CONV - 1):
            d = F_CONV - 1 - k
            sh = pltpu.roll(u, d, 0)
            for r in range(d):
                sh = jnp.where(row == r, uh[SUBLANE - d + r:SUBLANE - d + r + 1, :], sh)
            y = y + cw_ref[k:k + 1, c0:c0 + cw] * sh
        return y

    starts = [sum(widths[:n]) for n in range(len(widths))]
    both = lambda n: (conv_cols(starts[n], widths[n]), conv_cols(D_FF + starts[n], widths[n]))
    acc = x
    nxt = both(0)
    for n, (c0, cw) in enumerate(zip(starts, widths)):
        ya, yb = nxt
        if n + 1 < len(widths):
            nxt = both(n + 1)
        gated = _gelu(ya) * yb
        acc = acc + _dot(gated.astype(bf16), wdn_ref[c0:c0 + cw, :])
    o_ref[...] = _rms(acc, gf_ref[...])


def _ffn(x, g, w_up, conv_w, w_down, g_final, tm=1024, widths=(1280, 1536)):
    s = x.shape[0]
    halo = tm // SUBLANE
    assert sum(widths) == D_FF
    return pl.pallas_call(
        functools.partial(_ffn_kernel, tm=tm, widths=widths),
        grid=(s // tm,),
        in_specs=[pl.BlockSpec((tm, D_MODEL), lambda i: (i, 0)),
                  pl.BlockSpec((SUBLANE, D_MODEL), lambda i: (jnp.maximum(i * halo - 1, 0), 0)),
                  _const_spec((1, D_MODEL)),
                  _const_spec((D_MODEL, 2 * D_FF)),
                  _const_spec((F_CONV, 2 * D_FF)),
                  _const_spec((D_FF, D_MODEL)),
                  _const_spec((1, D_MODEL))],
        out_specs=pl.BlockSpec((tm, D_MODEL), lambda i: (i, 0)),
        out_shape=jax.ShapeDtypeStruct((s, D_MODEL), f32),
        compiler_params=_params("parallel"),
        name="ffn",
    )(x, x, g, w_up, conv_w, w_down, g_final)


def _pad_cols(a, n):
    return jnp.pad(a, ((0, 0), (0, n - a.shape[1])))


def _pad_heads(a, heads, dh):
    r = a.shape[0]
    return jnp.pad(a.reshape(r, heads, dh), ((0, 0), (0, 0), (0, LANE - dh))).reshape(r, heads * LANE)


def _build_w_in(w):
    o = 0
    parts = {}
    sizes = (("q", 512), ("kc", 128), ("vc", 128), ("ks", 128), ("vs", 128), ("kw", 128), ("vw", 128),
             ("gt", 24), ("mq", 512), ("mk", 512), ("mv", 512), ("mo", 512), ("mi", 4), ("mf", 4))
    for name, n in sizes:
        parts[name] = w[:, o:o + n]
        o += n
    gt = parts["gt"].reshape(D_MODEL, NSA_GROUPS, NSA_HPG * 3)
    cols = [_pad_heads(parts["q"], NSA_HEADS, NSA_DH), parts["ks"], parts["vs"], parts["kw"], parts["vw"],
            parts["mq"], parts["mk"], parts["mv"], parts["mo"], parts["kc"], parts["vc"],
            _pad_cols(gt[:, 0], LANE), _pad_cols(gt[:, 1], LANE),
            _pad_cols(jnp.concatenate([parts["mi"], parts["mf"]], axis=1), LANE)]
    return jnp.concatenate(cols, axis=1).astype(bf16)


def _block_diag2(w):
    z = jnp.zeros_like(w)
    return jnp.concatenate([jnp.concatenate([w, z], axis=-1), jnp.concatenate([z, w], axis=-1)], axis=-2)


def kernel(x, mem, g_mix, w_in, cmp_pos_k, cmp_w1_k, cmp_w2_k, cmp_pos_v, cmp_w1_v, cmp_w2_v, mlstm_conv_w, mlstm_b_i, mlstm_b_f, g_head_nsa, g_head_mlstm, w_out, g_mem_q, g_mem_kv, w_mem_q, w_mem_k, w_mem_v, w_mem_o, g_ffn, w_up, ffn_conv_w, w_down, g_final):
    b, s, _ = x.shape
    assert b == 1 and g_mix.shape[0] == 1
    sel_tk = 2048
    assert s % 8192 == 0 and s >= sel_tk + WINDOW
    l = 0
    xs = x[0]
    row = lambda a: a.reshape(1, -1)

    qa, kst, vs, kwt, vw, pb = _inproj(xs, row(g_mix[l]), _build_w_in(w_in[l]), sel_tk // SEL_BLOCK)

    n16 = s // CMP_STRIDE
    dup = lambda p: jnp.concatenate([p, p], axis=-1)
    pos = jnp.stack([dup(cmp_pos_k[l]), dup(cmp_pos_v[l])])
    w1 = _block_diag2(jnp.stack([cmp_w1_k[l], cmp_w1_v[l]]).reshape(2, CMP_LEN, NSA_DH, -1)).astype(bf16)
    w2 = _block_diag2(jnp.stack([cmp_w2_k[l], cmp_w2_v[l]])).astype(bf16)
    kct, vc = _compress(pb, pos, w1, w2)

    n_sel = s // SEL_BLOCK
    cstart = jnp.arange(n16)[:, None] * CMP_STRIDE
    sstart = jnp.arange(n_sel)[None, :] * SEL_BLOCK
    overlap = ((cstart < sstart + SEL_BLOCK) & (cstart + CMP_LEN > sstart)).astype(bf16)
    oc, sel = _cmp_attention(qa, kct, vc, overlap)

    gh_pad = _pad_heads(row(g_head_nsa[l]), NSA_HEADS, NSA_DH)
    nsa_n = _selwin_attention(qa, pb, oc, sel, gh_pad, kst, vs, kwt, vw, tk=sel_tk)

    bias_if = _pad_cols(jnp.concatenate([row(mlstm_b_i[l]), row(mlstm_b_f[l])], axis=1), LANE)
    ml = _mlstm(pb, mlstm_conv_w[l], bias_if, row(g_head_mlstm[l]))

    w_out_nsa = jnp.pad(w_out[l][:NSA_HEADS * NSA_DH].reshape(NSA_HEADS, NSA_DH, D_MODEL),
                        ((0, 0), (0, LANE - NSA_DH), (0, 0))).reshape(NSA_HEADS * LANE, D_MODEL).astype(bf16)
    w_out_m = w_out[l][NSA_HEADS * NSA_DH:].astype(bf16)
    k_mem, v_mem = _memkv(mem[0], row(g_mem_kv[l]), w_mem_k[l].astype(bf16), w_mem_v[l].astype(bf16))
    x2 = _mix_mem(xs, nsa_n, ml, w_out_nsa, w_out_m, row(g_mem_q[l]), w_mem_q[l].astype(bf16),
                  k_mem.T, v_mem, w_mem_o[l].astype(bf16))

    out = _ffn(x2, row(g_ffn[l]), w_up[l].astype(bf16), ffn_conv_w[l], w_down[l].astype(bf16), row(g_final))
    return out[None]
```

```python
import functools

import jax
import jax.numpy as jnp
from jax import lax
from jax.experimental import pallas as pl
from jax.experimental.pallas import tpu as pltpu

f32 = jnp.float32
bf16 = jnp.bfloat16

D_MODEL = 1024
NSA_HEADS = 8
NSA_GROUPS = 2
NSA_HPG = NSA_HEADS // NSA_GROUPS
NSA_DH = 64
CMP_LEN = 32
CMP_STRIDE = 16
SEL_BLOCK = 64
SEL_TOPK = 16
WINDOW = 512
FORCE_BONUS = 1.0e4
M_HEADS = 4
M_DH = 128
M_WIDTH = M_HEADS * M_DH
M_CONV = 4
M_CHUNK = 64
MEM_HEADS = 4
MEM_DH = D_MODEL // MEM_HEADS
D_FF = 2816
F_CONV = 3
EPS = 1e-6
NSA_QSCALE = NSA_DH ** -0.5 * 1.4426950408889634

LANE = 128
SUBLANE = 8
NEG = -1e30
VMEM_LIMIT = 56 * 1024 * 1024

A_Q, A_KS, A_VS, A_KW, A_VW, NA = 0, 1024, 1152, 1280, 1408, 1536
B_MQ, B_MV, B_MO, B_KC, B_VC, B_G0, B_IF, NB = 0, 1024, 1536, 2048, 2176, 2304, 2560, 2688


def _dot(a, b):
    return jnp.dot(a, b, preferred_element_type=f32)


def _rms(x, g):
    return x * lax.rsqrt(jnp.mean(x * x, axis=-1, keepdims=True) + EPS) * g


def _gelu(x):
    return 0.5 * x * (1.0 + lax.erf(x * 0.7071067811865476))


def _params(*sem):
    return pltpu.CompilerParams(dimension_semantics=sem, vmem_limit_bytes=VMEM_LIMIT)


def _const_spec(shape):
    nd = len(shape)
    return pl.BlockSpec(shape, lambda *_: (0,) * nd, pipeline_mode=pl.Buffered(1))


def _split_keys_t(ref, k, extra=None):
    kt = k.T
    n_extra = 0 if extra is None else extra.shape[0]
    for g in range(NSA_GROUPS):
        ref[g, 0:NSA_DH, :] = kt[g * NSA_DH:(g + 1) * NSA_DH].astype(bf16)
        if extra is not None:
            ref[g, NSA_DH:NSA_DH + n_extra, :] = extra
        ref[g, NSA_DH + n_extra:, :] = jnp.zeros((LANE - NSA_DH - n_extra, kt.shape[1]), bf16)


def _split_values(ref, v, ones):
    lane = lax.broadcasted_iota(jnp.int32, v.shape, 1)
    fill = jnp.where(lane == NSA_DH, 1.0, 0.0) if ones == "col" else 0.0
    for g in range(NSA_GROUPS):
        vg = v if g == 0 else pltpu.roll(v, LANE - g * NSA_DH, 1)
        ref[g, :, 0:LANE] = jnp.where(lane < NSA_DH, vg, fill).astype(bf16)
        if ones == "tile":
            ref[g, :, LANE:2 * LANE] = jnp.ones(v.shape, bf16)


def _inproj_kernel(x_ref, g_ref, w_ref, q_ref, kst_ref, vs_ref, kwt_ref, vw_ref, pb_ref, *, tm, nblk):
    h = _rms(x_ref[...], g_ref[...]).astype(bf16)
    q_ref[...] = (_dot(h, w_ref[:, A_Q:A_KS]) * NSA_QSCALE).astype(bf16)
    kv = _dot(h, w_ref[:, A_KS:NA])
    key = pl.program_id(0) * tm + lax.broadcasted_iota(jnp.int32, (nblk, tm), 1)
    rowj = lax.broadcasted_iota(jnp.int32, (nblk, tm), 0)
    ind = ((lax.shift_right_logical(key, 6) & (nblk - 1)) == rowj).astype(bf16)
    _split_keys_t(kst_ref, kv[:, 0:LANE], ind)
    _split_values(vs_ref, kv[:, LANE:2 * LANE], "col")
    _split_keys_t(kwt_ref, kv[:, 2 * LANE:3 * LANE])
    _split_values(vw_ref, kv[:, 3 * LANE:4 * LANE], "tile")
    for c0 in range(0, NB, 512):
        c1 = min(c0 + 512, NB)
        pb_ref[:, c0:c1] = _dot(h, w_ref[:, NA + c0:NA + c1])


def _inproj(x, g, w, nblk, tm=512):
    s = x.shape[0]
    assert nblk & (nblk - 1) == 0 and SEL_BLOCK == 64
    kspec = pl.BlockSpec((NSA_GROUPS, LANE, tm), lambda i: (0, 0, i))
    vspec = lambda w: pl.BlockSpec((NSA_GROUPS, tm, w), lambda i: (0, i, 0))
    kshape = jax.ShapeDtypeStruct((NSA_GROUPS, LANE, s), bf16)
    vshape = lambda w: jax.ShapeDtypeStruct((NSA_GROUPS, s, w), bf16)
    return pl.pallas_call(
        functools.partial(_inproj_kernel, tm=tm, nblk=nblk),
        grid=(s // tm,),
        in_specs=[pl.BlockSpec((tm, D_MODEL), lambda i: (i, 0)),
                  _const_spec((1, D_MODEL)),
                  _const_spec((D_MODEL, NA + NB))],
        out_specs=[pl.BlockSpec((tm, NSA_HEADS * LANE), lambda i: (i, 0)), kspec, vspec(LANE), kspec, vspec(2 * LANE),
                   pl.BlockSpec((tm, NB), lambda i: (i, 0))],
        out_shape=[jax.ShapeDtypeStruct((s, NSA_HEADS * LANE), bf16), kshape, vshape(LANE), kshape, vshape(2 * LANE),
                   jax.ShapeDtypeStruct((s, NB), f32)],
        compiler_params=_params("parallel"),
        name="inproj",
    )(x, g, w)


def _compress_kernel(kc_ref, vc_ref, pos_ref, w1_ref, w2_ref, kct_ref, vcp_ref):
    n = kc_ref.shape[0] // CMP_STRIDE

    def mlp(x_ref, a):
        top = bot = None
        for l in range(CMP_STRIDE):
            xl = x_ref[pl.ds(l, n, stride=CMP_STRIDE), :]
            t = _dot((xl + pos_ref[a, l:l + 1, :]).astype(bf16), w1_ref[a, l])
            b = _dot((xl + pos_ref[a, CMP_STRIDE + l:CMP_STRIDE + l + 1, :]).astype(bf16), w1_ref[a, CMP_STRIDE + l])
            top = t if top is None else top + t
            bot = b if bot is None else bot + b
        pre = top + pltpu.roll(bot, n - 1, 0)
        return _dot(_gelu(pre).astype(bf16), w2_ref[a])

    _split_keys_t(kct_ref, mlp(kc_ref, 0))
    _split_values(vcp_ref, mlp(vc_ref, 1), None)


def _compress(pb, pos, w1, w2):
    s = pb.shape[0]
    n = s // CMP_STRIDE
    return pl.pallas_call(
        _compress_kernel,
        grid=(1,),
        in_specs=[pl.BlockSpec((s, LANE), lambda i: (0, B_KC // LANE)),
                  pl.BlockSpec((s, LANE), lambda i: (0, B_VC // LANE)),
                  _const_spec(pos.shape), _const_spec(w1.shape), _const_spec(w2.shape)],
        out_specs=[pl.BlockSpec((NSA_GROUPS, LANE, n), lambda i: (0, 0, 0)),
                   pl.BlockSpec((NSA_GROUPS, n, LANE), lambda i: (0, 0, 0))],
        out_shape=[jax.ShapeDtypeStruct((NSA_GROUPS, LANE, n), bf16),
                   jax.ShapeDtypeStruct((NSA_GROUPS, n, LANE), bf16)],
        compiler_params=_params("arbitrary"),
        name="compress",
    )(pb, pb, pos, w1, w2)


def _cmp_kernel(q_ref, kct_ref, vc_ref, ov_ref, oc_ref, sel_ref, *, tq, steps, classes):
    nsel = ov_ref.shape[1]
    t0 = pl.program_id(0) * tq
    tpos = t0 + lax.broadcasted_iota(jnp.int32, (tq, 1), 0)
    any_vis = tpos >= CMP_LEN - 1
    cur = lax.shift_right_logical(tpos, 6)

    def body(ncmp, nsel_w):
        cend = lax.broadcasted_iota(jnp.int32, (1, ncmp), 1) * CMP_STRIDE + (CMP_LEN - 1)
        vis = cend <= tpos
        blk = lax.broadcasted_iota(jnp.int32, (1, nsel_w), 1)
        forced = (blk == 0) | (blk == cur) | (blk == cur - 1)
        scores = []
        qk = lambda h: _dot(q_ref[:, h * LANE:(h + 1) * LANE], kct_ref[h // NSA_HPG, :, 0:ncmp])
        ahead = 2
        pending = [qk(h) for h in range(ahead)]
        for g in range(NSA_GROUPS):
            psum = None
            for hh in range(NSA_HPG):
                h = g * NSA_HPG + hh
                c0 = h * LANE
                s = jnp.where(vis, pending.pop(0), NEG)
                if h + ahead < NSA_HEADS:
                    pending.append(qk(h + ahead))
                e = jnp.exp2(s - jnp.max(s, axis=-1, keepdims=True))
                inv = jnp.where(any_vis, 1.0 / jnp.sum(e, axis=-1, keepdims=True), 0.0)
                eb = e.astype(bf16)
                oc_ref[:, c0:c0 + LANE] = _dot(eb, vc_ref[g, 0:ncmp, :]) * inv
                part = _dot(eb, ov_ref[0:ncmp, 0:nsel_w]) * inv
                psum = part if psum is None else psum + part
            imp = psum
            scores.append(jnp.where(forced, -jnp.inf, jnp.where(blk <= cur, imp, -FORCE_BONUS)))
        score = jnp.concatenate(scores, axis=0).T
        blkt = lax.broadcasted_iota(jnp.int32, score.shape, 0).astype(f32)
        for _ in range(SEL_TOPK - 3):
            mx = jnp.max(score, axis=0, keepdims=True)
            j = jnp.min(jnp.where(score == mx, blkt, float(nsel_w)), axis=0, keepdims=True)
            score = jnp.where(blkt == j, -jnp.inf, score)
        selm = (score == -jnp.inf).astype(f32).T
        for g in range(NSA_GROUPS):
            sel_ref[g, :, 0:nsel_w] = selm[g * tq:(g + 1) * tq]
            if nsel_w < nsel:
                sel_ref[g, :, nsel_w:] = jnp.zeros((tq, nsel - nsel_w), f32)

    per = steps // classes
    for k in range(classes):
        @pl.when((pl.program_id(0) >= k * per) & (pl.program_id(0) < (k + 1) * per))
        def _(k=k):
            body(kct_ref.shape[2] * (k + 1) // classes, nsel * (k + 1) // classes)


def _cmp_attention(pa, kct, vc, ov, tq=512, classes=4):
    s = pa.shape[0]
    ncmp, nsel = ov.shape
    steps = s // tq
    assert steps % classes == 0 and (ncmp // classes) % LANE == 0 and (nsel // classes) % SUBLANE == 0
    return pl.pallas_call(
        functools.partial(_cmp_kernel, tq=tq, steps=steps, classes=classes),
        grid=(steps,),
        in_specs=[pl.BlockSpec((tq, NSA_HEADS * LANE), lambda i: (i, 0)),
                  _const_spec((NSA_GROUPS, LANE, ncmp)),
                  _const_spec((NSA_GROUPS, ncmp, LANE)),
                  _const_spec((ncmp, nsel))],
        out_specs=[pl.BlockSpec((tq, NSA_HEADS * LANE), lambda i: (i, 0)),
                   pl.BlockSpec((NSA_GROUPS, tq, nsel), lambda i: (0, i, 0))],
        out_shape=[jax.ShapeDtypeStruct((s, NSA_HEADS * LANE), f32),
                   jax.ShapeDtypeStruct((NSA_GROUPS, s, nsel), f32)],
        compiler_params=_params("parallel"),
        name="cmp_attn",
    )(pa, kct, vc, ov)


def _selwin_kernel(q_ref, oc_ref, sel_ref, gate_ref, gh_ref, kst_ref, vs_ref, kwt_ref, vw_ref, out_ref, *, tq, tk):
    nsel = sel_ref.shape[2]
    nblk = tk // SEL_BLOCK
    t0 = pl.program_id(1) * tq
    tpos = t0 + lax.broadcasted_iota(jnp.int32, (tq, 1), 0)
    q4 = [q_ref[:, hh * LANE:(hh + 1) * LANE] for hh in range(NSA_HPG)]
    sel = sel_ref[0]
    lane = lax.broadcasted_iota(jnp.int32, (tq, LANE), 1)
    bias_lanes = (lane >= NSA_DH) & (lane < NSA_DH + nblk)

    def q_biased(kt):
        sh = lax.rem(NSA_DH + nsel - kt * nblk, jnp.int32(nsel))
        bias = ((pltpu.roll(sel, sh, 1)[:, :LANE] - 1.0) * -NEG).astype(bf16)
        return [jnp.where(bias_lanes, bias, qh) for qh in q4]

    def attend(qb, st, width, carry, causal):
        kblk = kst_ref[0, :, pl.ds(st, width)]
        vblk = vs_ref[0, pl.ds(st, width), :]
        if causal:
            keep = st + lax.broadcasted_iota(jnp.int32, (1, width), 1) <= tpos
        out = []
        s_next = _dot(qb[0], kblk)
        for hh in range(NSA_HPG):
            m, acc = carry[hh]
            s = s_next
            if hh + 1 < NSA_HPG:
                s_next = _dot(qb[hh + 1], kblk)
            if causal:
                s = jnp.where(keep, s, NEG)
            mn = jnp.maximum(m, jnp.max(s, axis=-1, keepdims=True))
            p = jnp.exp2(s - mn)
            out.append((mn, jnp.exp2(m - mn) * acc + _dot(p.astype(bf16), vblk)))
        return tuple(out)

    last = t0 // tk
    init = tuple((jnp.full((tq, 1), NEG, f32), jnp.zeros((tq, LANE), f32)) for _ in range(NSA_HPG))
    carry = lax.fori_loop(
        0, last, lambda kt, c: attend(q_biased(kt), pl.multiple_of(kt * tk, tk), tk, c, False), init)
    qb_last = q_biased(last)
    base = last * tk
    carry = lax.fori_loop(
        0, (t0 - base) // tq,
        lambda j, c: attend(qb_last, pl.multiple_of(base + j * tq, tq), tq, c, False), carry)

    parts = 4
    hq = tq // parts
    rows = lambda r: slice(r * hq, (r + 1) * hq)
    kdiag = kst_ref[0, :, pl.ds(pl.multiple_of(t0, tq), tq)]
    vdiag = vs_ref[0, pl.ds(pl.multiple_of(t0, tq), tq), :]
    dmask = t0 + lax.broadcasted_iota(jnp.int32, (1, tq), 1) <= tpos
    win = []
    for r in range(parts):
        tp = t0 + r * hq + lax.broadcasted_iota(jnp.int32, (hq, 1), 0)
        st = pl.multiple_of(jnp.maximum(t0 + r * hq - WINDOW, 0), LANE)
        kpos = st + lax.broadcasted_iota(jnp.int32, (1, hq + WINDOW), 1)
        win.append((kwt_ref[0, :, pl.ds(st, hq + WINDOW)], vw_ref[0, pl.ds(st, hq + WINDOW), :],
                    (kpos <= tp) & (kpos > tp - WINDOW)))
    chains = [c for hh in range(NSA_HPG) for c in [("diag", hh, 0)] + [("win", hh, r) for r in range(parts)]]
    qk = lambda br, hh, r: _dot(qb_last[hh], kdiag) if br == "diag" else _dot(q4[hh][rows(r)], win[r][0])

    ngate = 3 * NSA_HPG
    erow = lax.broadcasted_iota(jnp.int32, (LANE, ngate * LANE), 0)
    ecol = lax.broadcasted_iota(jnp.int32, (LANE, ngate * LANE), 1)
    expand = (erow == lax.shift_right_logical(ecol, 7)).astype(bf16)
    rest = jax.nn.sigmoid(gate_ref[...])
    gates = None
    for _ in range(3):
        term = rest.astype(bf16)
        rest = rest - term.astype(f32)
        part = _dot(term, expand)
        gates = part if gates is None else gates + part
    gate = lambda hh, j, r: gates[rows(r), (3 * hh + j) * LANE:(3 * hh + j + 1) * LANE]

    valid = lax.broadcasted_iota(jnp.int32, (hq, LANE), 1) < NSA_DH
    ahead = 2
    pending = [qk(*c) for c in chains[:ahead]]
    o_sel = None
    for n, (br, hh, r) in enumerate(chains):
        s = pending.pop(0)
        if n + ahead < len(chains):
            pending.append(qk(*chains[n + ahead]))
        if br == "diag":
            m, acc = carry[hh]
            s = jnp.where(dmask, s, NEG)
            mn = jnp.maximum(m, jnp.max(s, axis=-1, keepdims=True))
            acc_s = jnp.exp2(m - mn) * acc + _dot(jnp.exp2(s - mn).astype(bf16), vdiag)
            o_sel = acc_s / acc_s[:, NSA_DH:NSA_DH + 1]
            continue
        _, vwin, wmask = win[r]
        s = jnp.where(wmask, s, NEG)
        pw = jnp.exp2(s - jnp.max(s, axis=-1, keepdims=True))
        acc_w = _dot(pw.astype(bf16), vwin)
        o_w = acc_w[:, :LANE] * (gate(hh, 2, r) / acc_w[:, LANE:])
        o = gate(hh, 0, r) * oc_ref[rows(r), hh * LANE:(hh + 1) * LANE] + gate(hh, 1, r) * o_sel[rows(r)] + o_w
        o = jnp.where(valid, o, 0.0)
        ms = jnp.sum(o * o, axis=-1, keepdims=True) * (1.0 / NSA_DH)
        y = o * lax.rsqrt(ms + EPS) * gh_ref[:, hh * LANE:(hh + 1) * LANE]
        out_ref[rows(r), hh * LANE:(hh + 1) * LANE] = y.astype(bf16)


def _selwin_attention(pa, pb, oc, sel, gh_pad, kst, vs, kwt, vw, tq=512, tk=2048):
    s = pa.shape[0]
    nsel = sel.shape[2]
    gw = NSA_HPG * LANE
    assert tk % tq == 0 and s % tk == 0
    kspec = pl.BlockSpec((1, LANE, s), lambda g, i: (g, 0, 0), pipeline_mode=pl.Buffered(1))
    vspec = lambda w: pl.BlockSpec((1, s, w), lambda g, i: (g, 0, 0), pipeline_mode=pl.Buffered(1))
    return pl.pallas_call(
        functools.partial(_selwin_kernel, tq=tq, tk=tk),
        grid=(NSA_GROUPS, s // tq),
        in_specs=[pl.BlockSpec((tq, gw), lambda g, i: (i, g)),
                  pl.BlockSpec((tq, gw), lambda g, i: (i, g)),
                  pl.BlockSpec((1, tq, nsel), lambda g, i: (g, i, 0)),
                  pl.BlockSpec((tq, LANE), lambda g, i: (i, B_G0 // LANE + g)),
                  pl.BlockSpec((1, gw), lambda g, i: (0, g)),
                  kspec, vspec(LANE), kspec, vspec(2 * LANE)],
        out_specs=pl.BlockSpec((tq, gw), lambda g, i: (i, g)),
        out_shape=jax.ShapeDtypeStruct((s, NSA_HEADS * LANE), bf16),
        compiler_params=_params("parallel", "parallel"),
        name="selwin_attn",
    )(pa, oc, sel, pb, gh_pad, kst, vs, kwt, vw)


def _mlstm_kernel(qk_ref, v_ref, og_ref, if_ref, cw_ref, bias_ref, gh_ref, out_ref, buf, cst, mst, *, tm):
    @pl.when(pl.program_id(0) == 0)
    def _():
        buf[0:SUBLANE, :] = jnp.zeros((SUBLANE, 2 * M_WIDTH), f32)
        cst[...] = jnp.zeros(cst.shape, f32)
        mst[...] = jnp.zeros(mst.shape, f32)

    qk = qk_ref[...]
    buf[SUBLANE:SUBLANE + tm, :] = qk
    conv = cw_ref[M_CONV - 1:M_CONV, :] * qk
    for k in range(M_CONV - 1):
        off = SUBLANE - (M_CONV - 1) + k
        conv = conv + cw_ref[k:k + 1, :] * buf[off:off + tm, :]
    buf[0:SUBLANE, :] = qk[tm - SUBLANE:tm]
    qkc = conv * jax.nn.sigmoid(conv)
    qs = (qkc[:, :M_WIDTH] * (M_DH ** -0.5)).astype(bf16)
    ks = qkc[:, M_WIDTH:]
    vb = v_ref[...].astype(bf16)
    ogate = jax.nn.sigmoid(og_ref[...])

    pre = if_ref[...] + bias_ref[...]
    lane = lax.broadcasted_iota(jnp.int32, (M_CHUNK, LANE), 1)
    rowi = lax.broadcasted_iota(jnp.int32, (M_CHUNK, LANE), 0)
    logf = jnp.minimum(pre, 0.0) - jnp.log1p(jnp.exp(-jnp.abs(pre)))
    tri = (lax.broadcasted_iota(jnp.int32, (M_CHUNK, M_CHUNK), 0)
           >= lax.broadcasted_iota(jnp.int32, (M_CHUNK, M_CHUNK), 1))
    ones_col = (lax.broadcasted_iota(jnp.int32, (M_CHUNK, LANE), 1) == 0).astype(bf16)

    tn = (((1,), (1,)), ((), ()))
    tt = (((0,), (0,)), ((), ()))

    def state_part(c):
        r0, r1 = c * M_CHUNK, (c + 1) * M_CHUNK
        cum = logf[r0:r1]
        sft = 1
        while sft < M_CHUNK:
            cum = cum + jnp.where(rowi >= sft, pltpu.roll(cum, sft, 0), 0.0)
            sft *= 2
        comb = jnp.where(lane < M_HEADS, pre[r0:r1], cum)
        comb_t = comb.T
        part = []
        for h in range(M_HEADS):
            hs = slice(h * M_DH, (h + 1) * M_DH)
            bcol = comb[:, M_HEADS + h:M_HEADS + h + 1]
            icol = comb[:, h:h + 1]
            mprev = mst[0:1, h:h + 1]
            qh = qs[r0:r1, hs]
            kh = ks[r0:r1, hs]
            v_aug = jnp.concatenate([vb[r0:r1, hs], ones_col], axis=1)
            qk_s = lax.dot_general(qh, kh.astype(bf16), tn, preferred_element_type=f32)
            inter = _dot(qh, cst[h].astype(bf16))
            b_last = bcol[M_CHUNK - 1:M_CHUNK, :]
            g_s = b_last - bcol + icol
            m_new = jnp.maximum(b_last + mprev, jnp.max(g_s, axis=0, keepdims=True))
            w_s = jnp.exp(g_s - m_new)
            decay = jnp.exp(b_last + mprev - m_new)
            upd = lax.dot_general((kh * w_s).astype(bf16), v_aug, tt, preferred_element_type=f32)
            cst[h] = decay * cst[h] + upd
            mst[0:1, h:h + 1] = m_new
            part.append((bcol, mprev, qk_s, inter, v_aug))
        return comb_t, part

    def output_part(c, comb_t, part):
        r0, r1 = c * M_CHUNK, (c + 1) * M_CHUNK
        for h in range(M_HEADS):
            hs = slice(h * M_DH, (h + 1) * M_DH)
            bcol, mprev, qk_s, inter, v_aug = part[h]
            brow = comb_t[M_HEADS + h:M_HEADS + h + 1, :]
            irow = comb_t[h:h + 1, :]
            dmat = jnp.where(tri, bcol - brow + irow, -jnp.inf)
            m_inter = bcol + mprev
            m_t = jnp.maximum(jnp.max(dmat, axis=-1, keepdims=True), m_inter)
            sc = qk_s * jnp.exp(dmat - m_t)
            tot = jnp.exp(m_inter - m_t) * inter + _dot(sc.astype(bf16), v_aug)
            den = tot[:, M_DH:M_DH + 1]
            hout = tot[:, :M_DH] / jnp.maximum(jnp.abs(den), jnp.exp(-m_t))
            hn = hout * lax.rsqrt(jnp.mean(hout * hout, axis=-1, keepdims=True) + EPS) * gh_ref[:, hs]
            out_ref[r0:r1, hs] = (ogate[r0:r1, hs] * hn).astype(bf16)

    nc = tm // M_CHUNK
    cur = state_part(0)
    for c in range(nc):
        nxt = state_part(c + 1) if c + 1 < nc else None
        output_part(c, *cur)
        cur = nxt


def _mlstm(pb, conv_w, bias_if, g_head, tm=512):
    s = pb.shape[0]
    return pl.pallas_call(
        functools.partial(_mlstm_kernel, tm=tm),
        grid=(s // tm,),
        in_specs=[pl.BlockSpec((tm, 2 * M_WIDTH), lambda i: (i, B_MQ // (2 * M_WIDTH))),
                  pl.BlockSpec((tm, M_WIDTH), lambda i: (i, B_MV // M_WIDTH)),
                  pl.BlockSpec((tm, M_WIDTH), lambda i: (i, B_MO // M_WIDTH)),
                  pl.BlockSpec((tm, LANE), lambda i: (i, B_IF // LANE)),
                  _const_spec((M_CONV, 2 * M_WIDTH)),
                  _const_spec((1, LANE)),
                  _const_spec((1, M_WIDTH))],
        out_specs=pl.BlockSpec((tm, M_WIDTH), lambda i: (i, 0)),
        out_shape=jax.ShapeDtypeStruct((s, M_WIDTH), bf16),
        scratch_shapes=[pltpu.VMEM((tm + SUBLANE, 2 * M_WIDTH), f32),
                        pltpu.VMEM((M_HEADS, M_DH, 2 * M_DH), f32),
                        pltpu.VMEM((SUBLANE, LANE), f32)],
        compiler_params=_params("arbitrary"),
        name="mlstm",
    )(pb, pb, pb, pb, conv_w, bias_if, g_head)


def _memkv_kernel(mem_ref, g_ref, wk_ref, wv_ref, k_ref, v_ref):
    hm = _rms(mem_ref[...], g_ref[...]).astype(bf16)
    k_ref[...] = _dot(hm, wk_ref[...]).astype(bf16)
    v_ref[...] = _dot(hm, wv_ref[...]).astype(bf16)


def _memkv(mem, g, wk, wv):
    m = mem.shape[0]
    return pl.pallas_call(
        _memkv_kernel,
        out_shape=[jax.ShapeDtypeStruct((m, D_MODEL), bf16)] * 2,
        compiler_params=pltpu.CompilerParams(vmem_limit_bytes=VMEM_LIMIT),
        name="mem_kv",
    )(mem, g, wk, wv)


def _mix_mem_kernel(x_ref, nsa_ref, ml_ref, won_ref, wom_ref, gq_ref, wq_ref, kt_ref, v_ref, wo_ref, o_ref):
    x1 = x_ref[...] + _dot(nsa_ref[...], won_ref[...]) + _dot(ml_ref[...], wom_ref[...])
    hq = _rms(x1, gq_ref[...]).astype(bf16)
    q = (_dot(hq, wq_ref[...]) * (MEM_DH ** -0.5)).astype(bf16)
    outs = []
    heads = [slice(h * MEM_DH, (h + 1) * MEM_DH) for h in range(MEM_HEADS)]
    scores = [_dot(q[:, hs], kt_ref[hs, :]) for hs in heads]
    for hs, s in zip(heads, scores):
        e = jnp.exp(s - jnp.max(s, axis=-1, keepdims=True))
        p = e / jnp.sum(e, axis=-1, keepdims=True)
        outs.append(_dot(p.astype(bf16), v_ref[:, hs]).astype(bf16))
    o_ref[...] = x1 + _dot(jnp.concatenate(outs, axis=1), wo_ref[...])


def _mix_mem(x, nsa_n, ml, w_out_nsa, w_out_m, g_q, w_q, k_t, v, w_o, tm=1024):
    s = x.shape[0]
    m = v.shape[0]
    return pl.pallas_call(
        _mix_mem_kernel,
        grid=(s // tm,),
        in_specs=[pl.BlockSpec((tm, D_MODEL), lambda i: (i, 0)),
                  pl.BlockSpec((tm, NSA_HEADS * LANE), lambda i: (i, 0)),
                  pl.BlockSpec((tm, M_WIDTH), lambda i: (i, 0)),
                  _const_spec((NSA_HEADS * LANE, D_MODEL)),
                  _const_spec((M_WIDTH, D_MODEL)),
                  _const_spec((1, D_MODEL)),
                  _const_spec((D_MODEL, D_MODEL)),
                  _const_spec((D_MODEL, m)),
                  _const_spec((m, D_MODEL)),
                  _const_spec((D_MODEL, D_MODEL))],
        out_specs=pl.BlockSpec((tm, D_MODEL), lambda i: (i, 0)),
        out_shape=jax.ShapeDtypeStruct((s, D_MODEL), f32),
        compiler_params=_params("parallel"),
        name="mix_mem",
    )(x, nsa_n, ml, w_out_nsa, w_out_m, g_q, w_q, k_t, v, w_o)


def _ffn_kernel(x_ref, xh_ref, g_ref, wup_ref, cw_ref, wdn_ref, gf_ref, o_ref, *, tm, widths):
    x = x_ref[...]
    hf = _rms(x, g_ref[...]).astype(bf16)
    hist = jnp.where(pl.program_id(0) > 0, 1.0, 0.0)
    hh = (_rms(xh_ref[...], g_ref[...]) * hist).astype(bf16)
    hall = jnp.concatenate([hh, hf], axis=0)
    nh = xh_ref.shape[0]

    def conv_cols(c0, cw):
        ua = _dot(hall, wup_ref[:, c0:c0 + cw])
        u = ua[nh:]
        uh = ua[nh - SUBLANE:nh]
        row = lax.broadcasted_iota(jnp.int32, (tm, cw), 0)
        y = cw_ref[F_CONV - 1:F_CONV, c0:c0 + cw] * u
        for k in range(F_CONV - 1):
            d = F_CONV - 1 - k
            sh = pltpu.roll(u, d, 0)
            for r in range(d):
                sh = jnp.where(row == r, uh[SUBLANE - d + r:SUBLANE - d + r + 1, :], sh)
            y = y + cw_ref[k:k + 1, c0:c0 + cw] * sh
        return y

    starts = [sum(widths[:n]) for n in range(len(widths))]
    both = lambda n: (conv_cols(starts[n], widths[n]), conv_cols(D_FF + starts[n], widths[n]))
    acc = x
    nxt = both(0)
    for n, (c0, cw) in enumerate(zip(starts, widths)):
        ya, yb = nxt
        if n + 1 < len(widths):
            nxt = both(n + 1)
        gated = _gelu(ya) * yb
        acc = acc + _dot(gated.astype(bf16), wdn_ref[c0:c0 + cw, :])
    o_ref[...] = _rms(acc, gf_ref[...])


def _ffn(x, g, w_up, conv_w, w_down, g_final, tm=1024, widths=(1280, 1536)):
    s = x.shape[0]
    hrows = 2 * SUBLANE
    halo = tm // hrows
    assert sum(widths) == D_FF
    return pl.pallas_call(
        functools.partial(_ffn_kernel, tm=tm, widths=widths),
        grid=(s // tm,),
        in_specs=[pl.BlockSpec((tm, D_MODEL), lambda i: (i, 0)),
                  pl.BlockSpec((hrows, D_MODEL), lambda i: (jnp.maximum(i * halo - 1, 0), 0)),
                  _const_spec((1, D_MODEL)),
                  _const_spec((D_MODEL, 2 * D_FF)),
                  _const_spec((F_CONV, 2 * D_FF)),
                  _const_spec((D_FF, D_MODEL)),
                  _const_spec((1, D_MODEL))],
        out_specs=pl.BlockSpec((tm, D_MODEL), lambda i: (i, 0)),
        out_shape=jax.ShapeDtypeStruct((s, D_MODEL), f32),
        compiler_params=_params("parallel"),
        name="ffn",
    )(x, x, g, w_up, conv_w, w_down, g_final)


def _pad_cols(a, n):
    return jnp.pad(a, ((0, 0), (0, n - a.shape[1])))


def _pad_heads(a, heads, dh):
    r = a.shape[0]
    return jnp.pad(a.reshape(r, heads, dh), ((0, 0), (0, 0), (0, LANE - dh))).reshape(r, heads * LANE)


def _build_w_in(w):
    o = 0
    parts = {}
    sizes = (("q", 512), ("kc", 128), ("vc", 128), ("ks", 128), ("vs", 128), ("kw", 128), ("vw", 128),
             ("gt", 24), ("mq", 512), ("mk", 512), ("mv", 512), ("mo", 512), ("mi", 4), ("mf", 4))
    for name, n in sizes:
        parts[name] = w[:, o:o + n]
        o += n
    gt = parts["gt"].reshape(D_MODEL, NSA_GROUPS, NSA_HPG * 3)
    cols = [_pad_heads(parts["q"], NSA_HEADS, NSA_DH), parts["ks"], parts["vs"], parts["kw"], parts["vw"],
            parts["mq"], parts["mk"], parts["mv"], parts["mo"], parts["kc"], parts["vc"],
            _pad_cols(gt[:, 0], LANE), _pad_cols(gt[:, 1], LANE),
            _pad_cols(jnp.concatenate([parts["mi"], parts["mf"]], axis=1), LANE)]
    return jnp.concatenate(cols, axis=1).astype(bf16)


def _block_diag2(w):
    z = jnp.zeros_like(w)
    return jnp.concatenate([jnp.concatenate([w, z], axis=-1), jnp.concatenate([z, w], axis=-1)], axis=-2)


def kernel(x, mem, g_mix, w_in, cmp_pos_k, cmp_w1_k, cmp_w2_k, cmp_pos_v, cmp_w1_v, cmp_w2_v, mlstm_conv_w, mlstm_b_i, mlstm_b_f, g_head_nsa, g_head_mlstm, w_out, g_mem_q, g_mem_kv, w_mem_q, w_mem_k, w_mem_v, w_mem_o, g_ffn, w_up, ffn_conv_w, w_down, g_final):
    b, s, _ = x.shape
    assert b == 1 and g_mix.shape[0] == 1
    sel_tk = 2048
    assert s % 8192 == 0 and s >= sel_tk + WINDOW
    l = 0
    xs = x[0]
    row = lambda a: a.reshape(1, -1)

    qa, kst, vs, kwt, vw, pb = _inproj(xs, row(g_mix[l]), _build_w_in(w_in[l]), sel_tk // SEL_BLOCK)

    n16 = s // CMP_STRIDE
    dup = lambda p: jnp.concatenate([p, p], axis=-1)
    pos = jnp.stack([dup(cmp_pos_k[l]), dup(cmp_pos_v[l])])
    w1 = _block_diag2(jnp.stack([cmp_w1_k[l], cmp_w1_v[l]]).reshape(2, CMP_LEN, NSA_DH, -1)).astype(bf16)
    w2 = _block_diag2(jnp.stack([cmp_w2_k[l], cmp_w2_v[l]])).astype(bf16)
    kct, vc = _compress(pb, pos, w1, w2)

    n_sel = s // SEL_BLOCK
    cstart = jnp.arange(n16)[:, None] * CMP_STRIDE
    sstart = jnp.arange(n_sel)[None, :] * SEL_BLOCK
    overlap = ((cstart < sstart + SEL_BLOCK) & (cstart + CMP_LEN > sstart)).astype(bf16)
    oc, sel = _cmp_attention(qa, kct, vc, overlap)

    gh_pad = _pad_heads(row(g_head_nsa[l]), NSA_HEADS, NSA_DH)
    nsa_n = _selwin_attention(qa, pb, oc, sel, gh_pad, kst, vs, kwt, vw, tk=sel_tk)

    bias_if = _pad_cols(jnp.concatenate([row(mlstm_b_i[l]), row(mlstm_b_f[l])], axis=1), LANE)
    ml = _mlstm(pb, mlstm_conv_w[l], bias_if, row(g_head_mlstm[l]))

    w_out_nsa = jnp.pad(w_out[l][:NSA_HEADS * NSA_DH].reshape(NSA_HEADS, NSA_DH, D_MODEL),
                        ((0, 0), (0, LANE - NSA_DH), (0, 0))).reshape(NSA_HEADS * LANE, D_MODEL).astype(bf16)
    w_out_m = w_out[l][NSA_HEADS * NSA_DH:].astype(bf16)
    k_mem, v_mem = _memkv(mem[0], row(g_mem_kv[l]), w_mem_k[l].astype(bf16), w_mem_v[l].astype(bf16))
    x2 = _mix_mem(xs, nsa_n, ml, w_out_nsa, w_out_m, row(g_mem_q[l]), w_mem_q[l].astype(bf16),
                  k_mem.T, v_mem, w_mem_o[l].astype(bf16))

    out = _ffn(x2, row(g_ffn[l]), w_up[l].astype(bf16), ffn_conv_w[l], w_down[l].astype(bf16), row(g_final))
    return out[None]
```
